```python
import math
import jax
import jax.numpy as jnp
from jax import lax

D_MODEL = 1024
BATCH = 16
SEQ = 4096
DEPTH = 4

N_A_LAYERS = DEPTH // 2
N_B_LAYERS = DEPTH - N_A_LAYERS
HG_HEAD_DIM = 128
HG_HEADS = D_MODEL // HG_HEAD_DIM
HG_CHUNK = 64
FORGET_FLOOR = 1e-30
MLA_HEADS = D_MODEL // 128
MLA_NOPE = 128
MLA_ROPE = 64
MLA_V = 128
MLA_Q_RANK = 384
MLA_KV_RANK = 256
ROPE_THETA = 10000.0
Q_BLOCK = 128
MASK_VALUE = -1e30
N_EXPERTS = 32
TOP_K = 4
EXPERT_FF = D_MODEL
SWIGLU_LIMIT = 7.0
SWIGLU_ALPHA = 1.702
MOE_BLOCK = 256
NORM_EPS = 1e-6
POS_OFFSET_MAX = 1024
ADA_INIT = 0.5

kernel_name = 'hybrid_hgrn2_mla_yoco_moe'


def rms_norm(t, gain):
    tf = t.astype(jnp.float32)
    tf = tf * lax.rsqrt(jnp.mean(tf * tf, axis=-1, keepdims=True) + NORM_EPS)
    return (tf * gain.astype(jnp.float32)).astype(t.dtype)


def modulate(t, shift, scale):
    return t * (1.0 + scale[:, None, :]) + shift[:, None, :]


def apply_rope(t, positions):
    half = MLA_ROPE // 2
    inv_freq = jnp.power(ROPE_THETA, -jnp.arange(half, dtype=jnp.float32) * (2.0 / MLA_ROPE))
    ang = positions.astype(jnp.float32)[..., None] * inv_freq
    if t.ndim == 4:
        ang = ang[:, :, None, :]
    cos, sin = jnp.cos(ang), jnp.sin(ang)
    tf = t.astype(jnp.float32)
    t1, t2 = tf[..., :half], tf[..., half:]
    return jnp.concatenate([t1 * cos - t2 * sin, t2 * cos + t1 * sin], axis=-1).astype(t.dtype)


def hgrn2_mixer(h, w_in, lower_bound, g_out, w_out):
    bsz, seq, _ = h.shape
    n_chunks = seq // HG_CHUNK
    q, f, i, g = jnp.split(h @ w_in, 4, axis=-1)
    lb = lower_bound.astype(jnp.float32)
    f = f.astype(jnp.float32)
    q = jax.nn.silu(q.astype(jnp.float32))
    forget = lb + (1.0 - lb) * jax.nn.sigmoid(f)
    log_forget = jnp.log(jnp.maximum(forget, FORGET_FLOOR))
    key = (1.0 - lb) * jax.nn.sigmoid(-f)

    def to_chunks(t):
        return t.reshape(bsz, n_chunks, HG_CHUNK, HG_HEADS, HG_HEAD_DIM).transpose(1, 0, 3, 2, 4)

    causal = jnp.tril(jnp.ones((HG_CHUNK, HG_CHUNK), dtype=bool))[:, :, None]

    def chunk_step(state, inp):
        qc, kc, vc, lc = inp
        cum = jnp.cumsum(lc, axis=2)
        o_inter = jnp.einsum('bhtk,bhkv->bhtv', qc * jnp.exp(cum), state)
        diff = cum[:, :, :, None, :] - cum[:, :, None, :, :]
        decay = jnp.where(causal, jnp.exp(jnp.where(causal, diff, 0.0)), 0.0)
        scores = jnp.einsum('bhtsk,bhsk->bhts', qc[:, :, :, None, :] * decay, kc)
        o_intra = jnp.einsum('bhts,bhsv->bhtv', scores, vc)
        last = cum[:, :, -1, :]
        new_state = jnp.exp(last)[..., None] * state + jnp.einsum(
            'bhsk,bhsv->bhkv', kc * jnp.exp(last[:, :, None, :] - cum), vc)
        return new_state, o_inter + o_intra

    init = jnp.zeros((bsz, HG_HEADS, HG_HEAD_DIM, HG_HEAD_DIM), jnp.float32)
    _, o = lax.scan(chunk_step, init, (to_chunks(q), to_chunks(key),
                                       to_chunks(i.astype(jnp.float32)), to_chunks(log_forget)))
    o = o.transpose(1, 0, 3, 2, 4).reshape(bsz, seq, HG_HEADS, HG_HEAD_DIM)
    gate = jax.nn.silu(g.astype(jnp.float32)).reshape(bsz, seq, HG_HEADS, HG_HEAD_DIM)
    o = rms_norm(o, g_out) * gate
    return o.reshape(bsz, seq, HG_HEADS * HG_HEAD_DIM).astype(h.dtype) @ w_out


def shared_latent_kv(hk, positions, w_dkv, g_ckv, w_ukv):
    bsz, seq, _ = hk.shape
    ckr = hk @ w_dkv
    c_kv = rms_norm(ckr[..., :MLA_KV_RANK], g_ckv)
    k_rope = apply_rope(ckr[..., MLA_KV_RANK:], positions)
    kv = (c_kv @ w_ukv).reshape(bsz, seq, MLA_HEADS, MLA_NOPE + MLA_V)
    return kv[..., :MLA_NOPE], kv[..., MLA_NOPE:], k_rope


def mla_mixer(h, positions, w_dq, g_cq, w_uq, w_o, k_nope, v, k_rope):
    bsz, seq, _ = h.shape
    cq = rms_norm(h @ w_dq, g_cq)
    q = (cq @ w_uq).reshape(bsz, seq, MLA_HEADS, MLA_NOPE + MLA_ROPE)
    q_nope = q[..., :MLA_NOPE]
    q_rope = apply_rope(q[..., MLA_NOPE:], positions)
    scale = 1.0 / math.sqrt(MLA_NOPE + MLA_ROPE)
    outs = []
    for blk in range(seq // Q_BLOCK):
        lo, hi = blk * Q_BLOCK, (blk + 1) * Q_BLOCK
        s = (jnp.einsum('bqhd,bkhd->bhqk', q_nope[:, lo:hi], k_nope[:, :hi])
             + jnp.einsum('bqhr,bkr->bhqk', q_rope[:, lo:hi], k_rope[:, :hi]))
        s = s.astype(jnp.float32) * scale
        mask = jnp.arange(hi)[None, :] <= jnp.arange(lo, hi)[:, None]
        p = jax.nn.softmax(jnp.where(mask, s, MASK_VALUE), axis=-1).astype(v.dtype)
        outs.append(jnp.einsum('bhqk,bkhd->bqhd', p, v[:, :hi]))
    o = jnp.concatenate(outs, axis=1).reshape(bsz, seq, MLA_HEADS * MLA_V)
    return o @ w_o


def moe_ffn(h, w_router, b_router, w_gu, b_gu, w_dn, b_dn):
    bsz, seq, dm = h.shape
    n_tok = bsz * seq
    xf = h.reshape(n_tok, dm)
    logits = (xf @ w_router).astype(jnp.float32) + b_router.astype(jnp.float32)
    top_vals, top_idx = lax.top_k(logits, TOP_K)
    weights = jax.nn.softmax(top_vals, axis=-1)
    n_assign = n_tok * TOP_K
    flat_e = top_idx.reshape(-1)
    order = jnp.argsort(flat_e, stable=True)
    sorted_e = flat_e[order]
    counts = jnp.bincount(flat_e, length=N_EXPERTS)
    padded = (counts + MOE_BLOCK - 1) // MOE_BLOCK * MOE_BLOCK
    start = jnp.cumsum(counts) - counts
    pad_end = jnp.cumsum(padded)
    pad_start = pad_end - padded
    dest_sorted = pad_start[sorted_e] + jnp.arange(n_assign) - start[sorted_e]
    dest = jnp.zeros((n_assign,), jnp.int32).at[order].set(dest_sorted.astype(jnp.int32))
    n_rows = n_assign + N_EXPERTS * MOE_BLOCK
    n_blocks = n_rows // MOE_BLOCK
    row_tok = jnp.full((n_rows,), n_tok, jnp.int32).at[dest].set(jnp.arange(n_assign, dtype=jnp.int32) // TOP_K)
    x_pad = jnp.concatenate([xf, jnp.zeros((1, dm), xf.dtype)], axis=0)
    xs = x_pad[row_tok].reshape(n_blocks, MOE_BLOCK, dm)
    block_e = jnp.minimum(jnp.searchsorted(pad_end, jnp.arange(n_blocks) * MOE_BLOCK, side='right'),
                          N_EXPERTS - 1)

    def expert_block(args):
        xb, e = args
        gu = xb @ w_gu[e] + b_gu[e]
        gate, up = gu[:, :EXPERT_FF], gu[:, EXPERT_FF:]
        gate = jnp.minimum(gate, SWIGLU_LIMIT)
        up = jnp.clip(up, -SWIGLU_LIMIT, SWIGLU_LIMIT)
        glu = gate * jax.nn.sigmoid(SWIGLU_ALPHA * gate)
        return ((up + 1.0) * glu) @ w_dn[e] + b_dn[e]

    ys = lax.map(expert_block, (xs, block_e)).reshape(n_rows, dm)
    y = jnp.einsum('nkd,nk->nd', ys[dest].reshape(n_tok, TOP_K, dm), weights.astype(ys.dtype))
    return y.reshape(bsz, seq, dm)


def setup_inputs(seed: int = 0) -> dict:
    key = jax.random.key(seed)
    ks = jax.random.split(key, 28)
    D = D_MODEL
    f32 = jnp.float32

    def nrm(k, shape, fan_in, s=1.0):
        return jax.random.normal(k, shape, f32) * (s * fan_in ** -0.5)

    def gain(k, shape):
        return 1.0 + 0.05 * jax.random.normal(k, shape, f32)

    def bias(k, shape, s=0.02):
        return s * jax.random.normal(k, shape, f32)

    x = jax.random.normal(ks[0], (BATCH, SEQ, D), f32)
    c = jax.random.normal(ks[1], (BATCH, D), f32)
    offsets = jax.random.randint(ks[2], (BATCH, 1), 0, POS_OFFSET_MAX, dtype=jnp.int32)
    positions = offsets + jnp.arange(SEQ, dtype=jnp.int32)[None, :]
    return {
        'x': x,
        'c': c,
        'positions': positions,
        'g_mix': gain(ks[3], (DEPTH, D)),
        'g_ffn': gain(ks[4], (DEPTH, D)),
        'w_ada': nrm(ks[5], (DEPTH, D, 6 * D), D, ADA_INIT),
        'b_ada': bias(ks[6], (DEPTH, 6 * D)),
        'w_in_a': nrm(ks[7], (N_A_LAYERS, D, 4 * D), D),
        'lb_logits': jax.random.normal(ks[8], (N_A_LAYERS, D), f32),
        'g_out_a': gain(ks[9], (N_A_LAYERS, HG_HEAD_DIM)),
        'w_out_a': nrm(ks[10], (N_A_LAYERS, D, D), D),
        'g_kv': gain(ks[11], (D,)),
        'w_ada_kv': nrm(ks[12], (D, 2 * D), D, ADA_INIT),
        'b_ada_kv': bias(ks[13], (2 * D,)),
        'w_dkv': nrm(ks[14], (D, MLA_KV_RANK + MLA_ROPE), D),
        'g_ckv': gain(ks[15], (MLA_KV_RANK,)),
        'w_ukv': nrm(ks[16], (MLA_KV_RANK, MLA_HEADS * (MLA_NOPE + MLA_V)), MLA_KV_RANK),
        'w_dq': nrm(ks[17], (N_B_LAYERS, D, MLA_Q_RANK), D),
        'g_cq': gain(ks[18], (N_B_LAYERS, MLA_Q_RANK)),
        'w_uq': nrm(ks[19], (N_B_LAYERS, MLA_Q_RANK, MLA_HEADS * (MLA_NOPE + MLA_ROPE)), MLA_Q_RANK),
        'w_o_b': nrm(ks[20], (N_B_LAYERS, MLA_HEADS * MLA_V, D), MLA_HEADS * MLA_V),
        'w_router': nrm(ks[21], (DEPTH, D, N_EXPERTS), D),
        'b_router': bias(ks[22], (DEPTH, N_EXPERTS), 0.01),
        'w_gu': nrm(ks[23], (DEPTH, N_EXPERTS, D, 2 * EXPERT_FF), D),
        'b_gu': bias(ks[24], (DEPTH, N_EXPERTS, 2 * EXPERT_FF)),
        'w_dn': nrm(ks[25], (DEPTH, N_EXPERTS, EXPERT_FF, D), EXPERT_FF),
        'b_dn': bias(ks[26], (DEPTH, N_EXPERTS, D)),
        'g_final': gain(ks[27], (D,)),
    }


def reference(x, c, positions, g_mix, g_ffn, w_ada, b_ada, w_in_a, lb_logits, g_out_a, w_out_a,
              g_kv, w_ada_kv, b_ada_kv, w_dkv, g_ckv, w_ukv, w_dq, g_cq, w_uq, w_o_b,
              w_router, b_router, w_gu, b_gu, w_dn, b_dn, g_final):
    c_act = jax.nn.silu(c)
    lb_p = jax.nn.softmax(lb_logits.astype(jnp.float32), axis=0)
    lower_bounds = jnp.cumsum(lb_p, axis=0) - lb_p[0]
    shared = None
    for l in range(DEPTH):
        mod = c_act @ w_ada[l] + b_ada[l]
        sh_m, sc_m, gt_m, sh_f, sc_f, gt_f = jnp.split(mod, 6, axis=-1)
        h = modulate(rms_norm(x, g_mix[l]), sh_m, sc_m)
        if l < N_A_LAYERS:
            y = hgrn2_mixer(h, w_in_a[l], lower_bounds[l], g_out_a[l], w_out_a[l])
        else:
            j = l - N_A_LAYERS
            y = mla_mixer(h, positions, w_dq[j], g_cq[j], w_uq[j], w_o_b[j], *shared)
        x = x + gt_m[:, None, :] * y
        h = modulate(rms_norm(x, g_ffn[l]), sh_f, sc_f)
        x = x + gt_f[:, None, :] * moe_ffn(h, w_router[l], b_router[l], w_gu[l], b_gu[l], w_dn[l], b_dn[l])
        if l == N_A_LAYERS - 1:
            sh_kv, sc_kv = jnp.split(c_act @ w_ada_kv + b_ada_kv, 2, axis=-1)
            hk = modulate(rms_norm(x, g_kv), sh_kv, sc_kv)
            shared = shared_latent_kv(hk, positions, w_dkv, g_ckv, w_ukv)
    return rms_norm(x, g_final)
```

```python
import functools
import math

import numpy as np
import jax
import jax.numpy as jnp
from jax import lax
from jax.experimental import pallas as pl
from jax.experimental.pallas import tpu as pltpu

F32 = jnp.float32
BF16 = jnp.bfloat16
I32 = jnp.int32

LANES = 128
SUBLANES = 8
VMEM_LIMIT = 48 * 1024 * 1024

HG_HEAD_DIM = 128
HG_CHUNK = 64
FORGET_FLOOR = 1e-30
MLA_NOPE = 128
MLA_ROPE = 64
MLA_V = 128
MLA_KV_RANK = 256
ROPE_THETA = 10000.0
N_EXPERTS = 32
TOP_K = 4
SWIGLU_LIMIT = 7.0
SWIGLU_ALPHA = 1.702
MOE_BLOCK = 256
NORM_EPS = 1e-6
NEG_BIG = -1e30

N_LEVELS = int(math.log2(HG_CHUNK))


def _params(sem, vmem=VMEM_LIMIT):
    return pltpu.CompilerParams(dimension_semantics=sem, vmem_limit_bytes=vmem)


def _dot(a, b):
    return jnp.dot(a, b, preferred_element_type=F32)


def _dot_nt(a, b):
    return lax.dot_general(a, b, (((1,), (1,)), ((), ())), preferred_element_type=F32)


def _split2(a):
    hi = a.astype(BF16)
    lo = (a - hi.astype(F32)).astype(BF16)
    return hi, lo


def _dot3(a, b):
    a_hi, a_lo = _split2(a)
    b_hi, b_lo = _split2(b)
    return _dot(a_hi, b_hi) + _dot(a_lo, b_hi) + _dot(a_hi, b_lo)


def _sigmoid(x):
    return 1.0 / (1.0 + jnp.exp(-x))


def _rms(x, gain):
    ms = jnp.mean(x * x, axis=-1, keepdims=True)
    return x * lax.rsqrt(ms + NORM_EPS) * gain


def _norm_mod(x, gain, shift, scale):
    return _rms(x, gain) * (1.0 + scale) + shift


def _ada_kernel(c_ref, w_ref, b_ref, o_ref):
    c = c_ref[...]
    c_act = c * _sigmoid(c)
    o_ref[0] = _dot3(c_act, w_ref[0]) + b_ref[0]


def _ada_mod(c, w, b):
    n_l, d, m = w.shape
    bsz = c.shape[0]
    tn = 1024
    return pl.pallas_call(
        _ada_kernel,
        grid=(n_l, m // tn),
        in_specs=[
            pl.BlockSpec((bsz, d), lambda l, j: (0, 0)),
            pl.BlockSpec((1, d, tn), lambda l, j: (l, 0, j)),
            pl.BlockSpec((1, 1, tn), lambda l, j: (l, 0, j)),
        ],
        out_specs=pl.BlockSpec((1, bsz, tn), lambda l, j: (l, 0, j)),
        out_shape=jax.ShapeDtypeStruct((n_l, bsz, m), F32),
        compiler_params=_params(("arbitrary", "arbitrary")),
        name="ada_mod",
    )(c, w, b.reshape(n_l, 1, m))


def _hgrn_in_kernel(layer, n_a, x_ref, mod_ref, g_ref, lbl_ref, w_ref,
                    q_ref, k_ref, v_ref, lf_ref, gate_ref):
    d = x_ref.shape[1]
    mod = mod_ref[0]
    h = _norm_mod(x_ref[...], g_ref[...], mod[0:1], mod[1:2]).astype(BF16)
    rows = [lbl_ref[i:i + 1, :] for i in range(n_a)]
    mx = functools.reduce(jnp.maximum, rows)
    ex = [jnp.exp(r - mx) for r in rows]
    den = functools.reduce(lambda a, b: a + b, ex)
    lb = (functools.reduce(lambda a, b: a + b, ex[:layer + 1]) - ex[0]) / den

    yq = _dot(h, w_ref[:, 0:d])
    q_ref[...] = (yq * _sigmoid(yq)).astype(BF16)
    yf = _dot(h, w_ref[:, d:2 * d])
    sg = _sigmoid(yf)
    forget = lb + (1.0 - lb) * sg
    lf_ref[...] = jnp.log(jnp.maximum(forget, FORGET_FLOOR))
    k_ref[...] = ((1.0 - lb) * (1.0 - sg)).astype(BF16)
    v_ref[...] = _dot(h, w_ref[:, 2 * d:3 * d]).astype(BF16)
    yg = _dot(h, w_ref[:, 3 * d:4 * d])
    gate_ref[...] = (yg * _sigmoid(yg)).astype(BF16)


def _hgrn_in(x2, mod, g_mix, lb_logits, w_in_bf, layer, seq):
    n, d = x2.shape
    tm = 512
    n_a = lb_logits.shape[0]
    tiles_per_b = seq // tm
    row = lambda i: (i, 0)
    return pl.pallas_call(
        functools.partial(_hgrn_in_kernel, layer, n_a),
        grid=(n // tm,),
        in_specs=[
            pl.BlockSpec((tm, d), row),
            pl.BlockSpec((1, 6, d), lambda i: (i // tiles_per_b, 0, 0)),
            pl.BlockSpec((1, d), lambda i: (0, 0)),
            pl.BlockSpec((n_a, d), lambda i: (0, 0)),
            pl.BlockSpec((d, 4 * d), lambda i: (0, 0)),
        ],
        out_specs=[pl.BlockSpec((tm, d), row)] * 5,
        out_shape=[jax.ShapeDtypeStruct((n, d), BF16)] * 3
        + [jax.ShapeDtypeStruct((n, d), F32), jax.ShapeDtypeStruct((n, d), BF16)],
        compiler_params=_params(("arbitrary",)),
        name="hgrn_in",
    )(x2, mod, g_mix.reshape(1, d), lb_logits, w_in_bf)


def _decay_matrix():
    c = HG_CHUNK
    m = np.zeros((N_LEVELS + 2, c, c), np.float32)
    for lvl in range(N_LEVELS):
        half = 1 << lvl
        for r in range(c):
            mid = (r // (2 * half)) * 2 * half + half - 1
            if (r // half) % 2 == 1:
                m[lvl, r, mid + 1:r + 1] = 1.0
            else:
                m[lvl, r, r + 1:mid + 1] = 1.0
    for r in range(c):
        m[N_LEVELS, r, :r + 1] = 1.0
        m[N_LEVELS + 1, r, r + 1:] = 1.0
    return m.reshape((N_LEVELS + 2) * c, c)


def _hgrn_rec_kernel(n_heads, x_ref, mod_ref, q_ref, k_ref, v_ref, lf_ref, gate_ref,
                     dm_ref, gout_ref, wout_ref, o_ref, state_ref, obuf_ref):
    c = HG_CHUNK
    hd = HG_HEAD_DIM
    ts = x_ref.shape[0]

    @pl.when(pl.program_id(1) == 0)
    def _():
        state_ref[...] = jnp.zeros_like(state_ref)

    t_idx = lax.broadcasted_iota(I32, (c, c), 0)
    s_idx = lax.broadcasted_iota(I32, (c, c), 1)
    r_idx = lax.broadcasted_iota(I32, (c, 1), 0)
    pair_masks = []
    right_rows = []
    for lvl in range(N_LEVELS):
        same_blk = (t_idx >> (lvl + 1)) == (s_idx >> (lvl + 1))
        t_right = ((t_idx >> lvl) & 1) == 1
        s_left = ((s_idx >> lvl) & 1) == 0
        pair_masks.append(jnp.where(same_blk, jnp.where(t_right, jnp.where(s_left, 1.0, 0.0), 0.0), 0.0))
        right_rows.append(((r_idx >> lvl) & 1) == 1)
    diag_mask = t_idx == s_idx
    dm = dm_ref[...]
    gout = gout_ref[...]

    def chunk_body(ci, carry):
        r0 = pl.multiple_of(ci * c, c)
        rows = pl.ds(r0, c)
        lf = lf_ref[rows, :]
        l_hi = lf.astype(BF16)
        rem = lf - l_hi.astype(F32)
        l_mid = rem.astype(BF16)
        l_lo = (rem - l_mid.astype(F32)).astype(BF16)
        dec = jnp.exp(_dot(dm, l_hi) + _dot(dm, l_mid) + _dot(dm, l_lo))
        for h in range(n_heads):
            cols = slice(h * hd, (h + 1) * hd)
            qh = q_ref[rows, cols].astype(F32)
            kh = k_ref[rows, cols].astype(F32)
            vh = v_ref[rows, cols].astype(F32)
            vb = vh.astype(BF16)
            sc = jnp.where(diag_mask, _dot_nt(qh.astype(BF16), kh.astype(BF16)), 0.0)
            for lvl in range(N_LEVELS):
                e = dec[lvl * c:(lvl + 1) * c, cols]
                xl = (jnp.where(right_rows[lvl], qh, kh) * e).astype(BF16)
                sc = sc + pair_masks[lvl] * _dot_nt(xl, xl)
            e_cum = dec[N_LEVELS * c:(N_LEVELS + 1) * c, cols]
            e_tail = dec[(N_LEVELS + 1) * c:(N_LEVELS + 2) * c, cols]
            st = state_ref[h]
            o = _dot_nt((qh * e_cum).astype(BF16), st.astype(BF16)) + _dot(sc.astype(BF16), vb)
            upd = _dot(vh.T.astype(BF16), (kh * e_tail).astype(BF16))
            state_ref[h] = st * e_cum[c - 1:c, :] + upd
            on = _rms(o, gout) * gate_ref[rows, cols].astype(F32)
            obuf_ref[rows, cols] = on.astype(BF16)
        return carry

    lax.fori_loop(0, ts // c, chunk_body, 0)
    y = _dot(obuf_ref[...], wout_ref[...])
    o_ref[...] = x_ref[...] + mod_ref[0][2:3] * y


def _hgrn_rec(x2, mod, q, k, v, lf, gate, g_out, w_out_bf, bsz, seq):
    n, d = x2.shape
    ts = 512
    n_heads = d // HG_HEAD_DIM
    tiles = seq // ts
    row = lambda b, j: (b * tiles + j, 0)
    const = lambda b, j: (0, 0)
    dm = jnp.asarray(_decay_matrix(), BF16)
    return pl.pallas_call(
        functools.partial(_hgrn_rec_kernel, n_heads),
        grid=(bsz, tiles),
        in_specs=[
            pl.BlockSpec((ts, d), row),
            pl.BlockSpec((1, 6, d), lambda b, j: (b, 0, 0)),
            pl.BlockSpec((ts, d), row), pl.BlockSpec((ts, d), row), pl.BlockSpec((ts, d), row),
            pl.BlockSpec((ts, d), row), pl.BlockSpec((ts, d), row),
            pl.BlockSpec(dm.shape, const),
            pl.BlockSpec((1, HG_HEAD_DIM), const),
            pl.BlockSpec((d, d), const),
        ],
        out_specs=pl.BlockSpec((ts, d), row),
        out_shape=jax.ShapeDtypeStruct((n, d), F32),
        scratch_shapes=[
            pltpu.VMEM((n_heads, HG_HEAD_DIM, HG_HEAD_DIM), F32),
            pltpu.VMEM((ts, d), BF16),
        ],
        compiler_params=_params(("arbitrary", "arbitrary")),
        name="hgrn_rec",
    )(x2, mod, q, k, v, lf, gate, dm, g_out.reshape(1, HG_HEAD_DIM), w_out_bf)


def _store_token_tiles(ref, val):
    tm = val.shape[0]
    for cidx in range(val.shape[1] // LANES):
        ref[pl.ds(cidx, tm, stride=SUBLANES), :] = val[:, cidx * LANES:(cidx + 1) * LANES]


def _load_token_tiles(ref, tm):
    n_c = ref.shape[0] // tm
    return jnp.concatenate([ref[pl.ds(cidx, tm, stride=SUBLANES), :] for cidx in range(n_c)], axis=-1)


def _router_kernel(x_ref, mod_ref, g_ref, wr_ref, br_ref, h_ref, idx_ref, wt_ref, cnt_ref):
    tm = x_ref.shape[0]
    mod = mod_ref[0]
    h = _norm_mod(x_ref[...], g_ref[...], mod[3:4], mod[4:5])
    _store_token_tiles(h_ref, h)
    logits = _dot3(h, wr_ref[...]) + br_ref[...]
    lane = lax.broadcasted_iota(I32, (tm, LANES), 1)
    lane_f = lane.astype(F32)
    idx_acc = jnp.zeros((tm, LANES), F32)
    val_acc = jnp.zeros((tm, LANES), F32)
    cnt = jnp.zeros((tm, LANES), F32)
    work = logits
    for kk in range(TOP_K):
        m = jnp.max(work, axis=-1, keepdims=True)
        sel = jnp.min(jnp.where(work == m, lane_f, float(LANES)), axis=-1, keepdims=True)
        hit = lane_f == sel
        work = jnp.where(hit, -jnp.inf, work)
        cnt = cnt + jnp.where(hit, 1.0, 0.0)
        idx_acc = jnp.where(lane == kk, sel, idx_acc)
        val_acc = jnp.where(lane == kk, m, val_acc)
    top = jnp.max(jnp.where(lane < TOP_K, val_acc, -jnp.inf), axis=-1, keepdims=True)
    ex = jnp.where(lane < TOP_K, jnp.exp(val_acc - top), 0.0)
    wts = ex / jnp.sum(ex, axis=-1, keepdims=True)
    idx_ref[...] = idx_acc[:, :TOP_K].astype(I32)
    wt_ref[...] = wts[:, :TOP_K]

    @pl.when(pl.program_id(0) == 0)
    def _():
        cnt_ref[...] = jnp.zeros_like(cnt_ref)

    cnt_ref[...] += jnp.sum(cnt, axis=0, keepdims=True)


def _moe_router(x2, mod, g_ffn, w_router, b_router, seq):
    n, d = x2.shape
    tm = 512
    tiles_per_b = seq // tm
    wr = jnp.zeros((d, LANES), F32).at[:, :N_EXPERTS].set(w_router)
    br = jnp.full((1, LANES), NEG_BIG, F32).at[0, :N_EXPERTS].set(b_router)
    return pl.pallas_call(
        _router_kernel,
        grid=(n // tm,),
        in_specs=[
            pl.BlockSpec((tm, d), lambda i: (i, 0)),
            pl.BlockSpec((1, 6, d), lambda i: (i // tiles_per_b, 0, 0)),
            pl.BlockSpec((1, d), lambda i: (0, 0)),
            pl.BlockSpec((d, LANES), lambda i: (0, 0)),
            pl.BlockSpec((1, LANES), lambda i: (0, 0)),
        ],
        out_specs=[
            pl.BlockSpec((tm * SUBLANES, LANES), lambda i: (i, 0)),
            pl.BlockSpec((tm, TOP_K), lambda i: (i, 0)),
            pl.BlockSpec((tm, TOP_K), lambda i: (i, 0)),
            pl.BlockSpec((1, LANES), lambda i: (0, 0)),
        ],
        out_shape=[
            jax.ShapeDtypeStruct((n * SUBLANES, LANES), F32),
            jax.ShapeDtypeStruct((n, TOP_K), I32),
            jax.ShapeDtypeStruct((n, TOP_K), F32),
            jax.ShapeDtypeStruct((1, LANES), F32),
        ],
        compiler_params=_params(("arbitrary",)),
        name="moe_router",
    )(x2, mod, g_ffn.reshape(1, d), wr, br)


def _dest_kernel(idx_ref, start_ref, tri_ref, dest_ref, carry_ref):
    tm = idx_ref.shape[0]

    @pl.when(pl.program_id(0) == 0)
    def _():
        carry_ref[...] = jnp.zeros_like(carry_ref)

    lane = lax.broadcasted_iota(I32, (tm, LANES), 1)
    idx = idx_ref[...]
    hits = [lane == idx[:, kk:kk + 1] for kk in range(TOP_K)]
    cnt = functools.reduce(lambda a, b: a + b, [jnp.where(hh, 1.0, 0.0) for hh in hits])
    before = _dot(tri_ref[...], cnt.astype(BF16))
    pos = before + carry_ref[...] + start_ref[...]
    acc = jnp.zeros((tm, LANES), F32)
    for kk in range(TOP_K):
        dk = jnp.sum(jnp.where(hits[kk], pos, 0.0), axis=-1, keepdims=True)
        acc = jnp.where(lane == kk, dk, acc)
    dest_ref[...] = acc[:, :TOP_K].astype(I32)
    carry_ref[...] += jnp.sum(cnt, axis=0, keepdims=True)


def _moe_dest(idx, pad_start):
    n = idx.shape[0]
    tm = 512
    tri = jnp.asarray(np.tril(np.ones((tm, tm), np.float32), -1), BF16)
    return pl.pallas_call(
        _dest_kernel,
        grid=(n // tm,),
        in_specs=[
            pl.BlockSpec((tm, TOP_K), lambda i: (i, 0)),
            pl.BlockSpec((1, LANES), lambda i: (0, 0)),
            pl.BlockSpec((tm, tm), lambda i: (0, 0)),
        ],
        out_specs=pl.BlockSpec((tm, TOP_K), lambda i: (i, 0)),
        out_shape=jax.ShapeDtypeStruct((n, TOP_K), I32),
        scratch_shapes=[pltpu.VMEM((1, LANES), F32)],
        compiler_params=_params(("arbitrary",)),
        name="moe_dest",
    )(idx, pad_start, tri)


DISPATCH_TOKENS = 256


def _token_copy(src, src_tok, dst, dst_tok, sem):
    s0 = pl.multiple_of(src_tok * SUBLANES, SUBLANES)
    d0 = pl.multiple_of(dst_tok * SUBLANES, SUBLANES)
    return pltpu.make_async_copy(src.at[pl.ds(s0, SUBLANES), :], dst.at[pl.ds(d0, SUBLANES), :], sem)


def _dispatch_kernel(dest_ref, h_ref, xs_in_ref, xs_ref, sem):
    del xs_in_ref
    t = DISPATCH_TOKENS

    def start(tok, carry):
        for kk in range(TOP_K):
            _token_copy(h_ref, tok, xs_ref, dest_ref[tok * TOP_K + kk], sem).start()
        return carry

    lax.fori_loop(0, t, start, 0)
    for _ in range(TOP_K):
        pltpu.make_async_copy(h_ref, xs_ref.at[pl.ds(0, t * SUBLANES), :], sem).wait()


def _moe_dispatch(h_tt, dest, n_rows):
    n = dest.shape[0]
    t = DISPATCH_TOKENS
    xs0 = jnp.zeros((n_rows * SUBLANES, LANES), F32)
    return pl.pallas_call(
        _dispatch_kernel,
        grid=(n // t,),
        in_specs=[
            pl.BlockSpec((t * TOP_K,), lambda i: (i,), memory_space=pltpu.SMEM),
            pl.BlockSpec((t * SUBLANES, LANES), lambda i: (i, 0)),
            pl.BlockSpec(memory_space=pl.ANY),
        ],
        out_specs=pl.BlockSpec(memory_space=pl.ANY),
        out_shape=jax.ShapeDtypeStruct(xs0.shape, F32),
        scratch_shapes=[pltpu.SemaphoreType.DMA(())],
        input_output_aliases={2: 0},
        compiler_params=_params(("arbitrary",)),
        name="moe_dispatch",
    )(dest.reshape(n * TOP_K), h_tt, xs0)


def _experts_kernel(be_ref, na_ref, xs_ref, wgu_ref, bgu_ref, wdn_ref, bdn_ref, ys_ref):
    i = pl.program_id(0)
    tm = MOE_BLOCK
    ff = wdn_ref.shape[1]

    @pl.when(i < na_ref[0])
    def _():
        xb = _load_token_tiles(xs_ref, tm).astype(BF16)
        gu = _dot(xb, wgu_ref[0]) + bgu_ref[0]
        gate = jnp.minimum(gu[:, :ff], SWIGLU_LIMIT)
        up = jnp.clip(gu[:, ff:], -SWIGLU_LIMIT, SWIGLU_LIMIT)
        glu = gate * _sigmoid(SWIGLU_ALPHA * gate)
        y = _dot(((up + 1.0) * glu).astype(BF16), wdn_ref[0]) + bdn_ref[0]
        _store_token_tiles(ys_ref, y)

    @pl.when(i >= na_ref[0])
    def _():
        ys_ref[...] = jnp.zeros_like(ys_ref)


def _moe_experts(xs, block_e, n_active, w_gu_bf, b_gu, w_dn_bf, b_dn):
    n_e, d, ff2 = w_gu_bf.shape
    ff = ff2 // 2
    n_blocks = block_e.shape[0]
    tm = MOE_BLOCK
    grid_spec = pltpu.PrefetchScalarGridSpec(
        num_scalar_prefetch=2,
        grid=(n_blocks,),
        in_specs=[
            pl.BlockSpec((tm * SUBLANES, LANES), lambda i, be, na: (i, 0)),
            pl.BlockSpec((1, d, ff2), lambda i, be, na: (be[i], 0, 0)),
            pl.BlockSpec((1, 1, ff2), lambda i, be, na: (be[i], 0, 0)),
            pl.BlockSpec((1, ff, d), lambda i, be, na: (be[i], 0, 0)),
            pl.BlockSpec((1, 1, d), lambda i, be, na: (be[i], 0, 0)),
        ],
        out_specs=pl.BlockSpec((tm * SUBLANES, LANES), lambda i, be, na: (i, 0)),
    )
    return pl.pallas_call(
        _experts_kernel,
        grid_spec=grid_spec,
        out_shape=jax.ShapeDtypeStruct(xs.shape, F32),
        compiler_params=_params(("arbitrary",)),
        name="moe_experts",
    )(block_e, n_active, xs, w_gu_bf, b_gu.reshape(n_e, 1, ff2), w_dn_bf, b_dn.reshape(n_e, 1, d))


def _combine_kernel(dest_ref, x_ref, mod_ref, wt_ref, ys_ref, o_ref, buf_ref, sem):
    t = DISPATCH_TOKENS
    slot_rows = t * SUBLANES

    def start(tok, carry):
        for kk in range(TOP_K):
            _token_copy(ys_ref, dest_ref[tok * TOP_K + kk], buf_ref, kk * t + tok, sem).start()
        return carry

    lax.fori_loop(0, t, start, 0)
    pltpu.make_async_copy(ys_ref.at[pl.ds(0, TOP_K * slot_rows), :], buf_ref, sem).wait()

    wt = wt_ref[...]
    y = jnp.zeros(x_ref.shape, F32)
    for kk in range(TOP_K):
        y = y + _load_token_tiles(buf_ref.at[pl.ds(kk * slot_rows, slot_rows), :], t) * wt[:, kk:kk + 1]
    o_ref[...] = x_ref[...] + mod_ref[0][5:6] * y


def _moe_combine(x2, mod, wts, dest, ys, seq):
    n, d = x2.shape
    t = DISPATCH_TOKENS
    tiles_per_b = seq // t
    return pl.pallas_call(
        _combine_kernel,
        grid=(n // t,),
        in_specs=[
            pl.BlockSpec((t * TOP_K,), lambda i: (i,), memory_space=pltpu.SMEM),
            pl.BlockSpec((t, d), lambda i: (i, 0)),
            pl.BlockSpec((1, 6, d), lambda i: (i // tiles_per_b, 0, 0)),
            pl.BlockSpec((t, TOP_K), lambda i: (i, 0)),
            pl.BlockSpec(memory_space=pl.ANY),
        ],
        out_specs=pl.BlockSpec((t, d), lambda i: (i, 0)),
        out_shape=jax.ShapeDtypeStruct((n, d), F32),
        scratch_shapes=[
            pltpu.VMEM((TOP_K * t * SUBLANES, LANES), F32),
            pltpu.SemaphoreType.DMA(()),
        ],
        compiler_params=_params(("arbitrary",)),
        name="moe_combine",
    )(dest.reshape(n * TOP_K), x2, mod, wts, ys)


def _moe_layer(x2, mod, g_ffn, w_router, b_router, w_gu_bf, b_gu, w_dn_bf, b_dn, seq):
    n = x2.shape[0]
    h_tt, idx, wts, counts = _moe_router(x2, mod, g_ffn, w_router, b_router, seq)
    counts = counts[0, :N_EXPERTS].astype(I32)
    padded = (counts + MOE_BLOCK - 1) // MOE_BLOCK * MOE_BLOCK
    pad_end = jnp.cumsum(padded)
    pad_start = pad_end - padded
    n_rows = n * TOP_K + N_EXPERTS * MOE_BLOCK
    n_blocks = n_rows // MOE_BLOCK
    block_e = jnp.minimum(
        jnp.searchsorted(pad_end, jnp.arange(n_blocks, dtype=I32) * MOE_BLOCK, side='right'),
        N_EXPERTS - 1).astype(I32)
    n_active = (pad_end[-1:] // MOE_BLOCK).astype(I32)
    start_vec = jnp.zeros((1, LANES), F32).at[0, :N_EXPERTS].set(pad_start.astype(F32))
    dest = _moe_dest(idx, start_vec)
    xs = _moe_dispatch(h_tt, dest, n_rows)
    ys = _moe_experts(xs, block_e, n_active, w_gu_bf, b_gu, w_dn_bf, b_dn)
    return _moe_combine(x2, mod, wts, dest, ys, seq)


def _rope_kernel(pos_ref, inv_ref, cos_ref, sin_ref):
    ang = pos_ref[...] * inv_ref[...]
    cos_ref[...] = jnp.cos(ang)
    sin_ref[...] = jnp.sin(ang)


def _rope_table(positions):
    half = MLA_ROPE // 2
    n = positions.size
    rep = LANES // half
    inv = np.power(np.float32(ROPE_THETA),
                   -np.arange(half, dtype=np.float32) * np.float32(2.0 / MLA_ROPE)).astype(np.float32)
    inv_row = jnp.asarray(np.tile(inv, rep).reshape(1, LANES))
    pos = jnp.repeat(positions.reshape(n).astype(F32), half).reshape(n // rep, LANES)
    rows = n // rep
    tr = min(rows, 2048)
    cos, sin = pl.pallas_call(
        _rope_kernel,
        grid=(rows // tr,),
        in_specs=[pl.BlockSpec((tr, LANES), lambda i: (i, 0)), pl.BlockSpec((1, LANES), lambda i: (0, 0))],
        out_specs=[pl.BlockSpec((tr, LANES), lambda i: (i, 0))] * 2,
        out_shape=[jax.ShapeDtypeStruct((rows, LANES), F32)] * 2,
        compiler_params=_params(("arbitrary",)),
        name="rope_table",
    )(pos, inv_row)
    return cos.reshape(n, half), sin.reshape(n, half)


def _rope_halves(t1, t2, cos, sin):
    return t1 * cos - t2 * sin, t2 * cos + t1 * sin


def _kv_kernel(n_heads, x_ref, mod_ref, g_ref, wd_ref, gc_ref, wu_ref, cos_ref, sin_ref, k_ref, v_ref):
    half = MLA_ROPE // 2
    mod = mod_ref[0]
    hk = _norm_mod(x_ref[...], g_ref[...], mod[0:1], mod[1:2]).astype(BF16)
    ckr = _dot(hk, wd_ref[...])
    c_kv = _rms(ckr[:, :MLA_KV_RANK], gc_ref[...]).astype(BF16)
    r1, r2 = _rope_halves(ckr[:, MLA_KV_RANK:MLA_KV_RANK + half], ckr[:, MLA_KV_RANK + half:],
                          cos_ref[...], sin_ref[...])
    kv = _dot(c_kv, wu_ref[...])
    nv = n_heads * MLA_NOPE
    for h in range(n_heads):
        k_ref[0, h] = jnp.concatenate(
            [kv[:, h * MLA_NOPE:(h + 1) * MLA_NOPE], r1, r2], axis=-1).astype(BF16)
        v_ref[0, h] = kv[:, nv + h * MLA_V:nv + (h + 1) * MLA_V].astype(BF16)


def _kv_shared(x2, mod_kv, g_kv, w_dkv, g_ckv, w_ukv, cos, sin, bsz, seq):
    n, d = x2.shape
    n_heads = w_ukv.shape[1] // (MLA_NOPE + MLA_V)
    half = MLA_ROPE // 2
    tm = 512
    tiles = seq // tm
    w3 = w_ukv.reshape(MLA_KV_RANK, n_heads, MLA_NOPE + MLA_V)
    wu = jnp.concatenate([w3[:, :, :MLA_NOPE].reshape(MLA_KV_RANK, -1),
                          w3[:, :, MLA_NOPE:].reshape(MLA_KV_RANK, -1)], axis=1).astype(BF16)
    row = lambda b, j: (b * tiles + j, 0)
    const = lambda b, j: (0, 0)
    qk = MLA_NOPE + MLA_ROPE
    return pl.pallas_call(
        functools.partial(_kv_kernel, n_heads),
        grid=(bsz, tiles),
        in_specs=[
            pl.BlockSpec((tm, d), row),
            pl.BlockSpec((1, 2, d), lambda b, j: (b, 0, 0)),
            pl.BlockSpec((1, d), const),
            pl.BlockSpec(w_dkv.shape, const),
            pl.BlockSpec((1, MLA_KV_RANK), const),
            pl.BlockSpec(wu.shape, const),
            pl.BlockSpec((tm, half), row),
            pl.BlockSpec((tm, half), row),
        ],
        out_specs=[
            pl.BlockSpec((1, n_heads, tm, qk), lambda b, j: (b, 0, j, 0)),
            pl.BlockSpec((1, n_heads, tm, MLA_V), lambda b, j: (b, 0, j, 0)),
        ],
        out_shape=[
            jax.ShapeDtypeStruct((bsz, n_heads, seq, qk), BF16),
            jax.ShapeDtypeStruct((bsz, n_heads, seq, MLA_V), BF16),
        ],
        compiler_params=_params(("arbitrary", "arbitrary")),
        name="kv_shared",
    )(x2, mod_kv, g_kv.reshape(1, d), w_dkv.astype(BF16), g_ckv.reshape(1, MLA_KV_RANK), wu, cos, sin)


def _mla_q_kernel(n_heads, x_ref, mod_ref, g_ref, wd_ref, gc_ref, wu_ref, cos_ref, sin_ref, q_ref):
    half = MLA_ROPE // 2
    mod = mod_ref[0]
    h = _norm_mod(x_ref[...], g_ref[...], mod[0:1], mod[1:2]).astype(BF16)
    cq = _rms(_dot(h, wd_ref[...]), gc_ref[...]).astype(BF16)
    q = _dot(cq, wu_ref[...])
    n1 = n_heads * MLA_NOPE
    n2 = n1 + n_heads * half
    cos = jnp.concatenate([cos_ref[...]] * n_heads, axis=-1)
    sin = jnp.concatenate([sin_ref[...]] * n_heads, axis=-1)
    r1, r2 = _rope_halves(q[:, n1:n2], q[:, n2:], cos, sin)
    scale = 1.0 / math.sqrt(MLA_NOPE + MLA_ROPE)
    for hh in range(n_heads):
        q_ref[0, hh] = (jnp.concatenate(
            [q[:, hh * MLA_NOPE:(hh + 1) * MLA_NOPE],
             r1[:, hh * half:(hh + 1) * half], r2[:, hh * half:(hh + 1) * half]], axis=-1) * scale).astype(BF16)


def _mla_q(x2, mod, g_mix, w_dq, g_cq, w_uq, cos, sin, bsz, seq):
    n, d = x2.shape
    q_rank = w_dq.shape[1]
    qk = MLA_NOPE + MLA_ROPE
    n_heads = w_uq.shape[1] // qk
    half = MLA_ROPE // 2
    tm = 512
    tiles = seq // tm
    w3 = w_uq.reshape(q_rank, n_heads, qk)
    wu = jnp.concatenate([w3[:, :, :MLA_NOPE].reshape(q_rank, -1),
                          w3[:, :, MLA_NOPE:MLA_NOPE + half].reshape(q_rank, -1),
                          w3[:, :, MLA_NOPE + half:].reshape(q_rank, -1)], axis=1).astype(BF16)
    row = lambda b, j: (b * tiles + j, 0)
    const = lambda b, j: (0, 0)
    return pl.pallas_call(
        functools.partial(_mla_q_kernel, n_heads),
        grid=(bsz, tiles),
        in_specs=[
            pl.BlockSpec((tm, d), row),
            pl.BlockSpec((1, 6, d), lambda b, j: (b, 0, 0)),
            pl.BlockSpec((1, d), const),
            pl.BlockSpec(w_dq.shape, const),
            pl.BlockSpec((1, q_rank), const),
            pl.BlockSpec(wu.shape, const),
            pl.BlockSpec((tm, half), row),
            pl.BlockSpec((tm, half), row),
        ],
        out_specs=pl.BlockSpec((1, n_heads, tm, qk), lambda b, j: (b, 0, j, 0)),
        out_shape=jax.ShapeDtypeStruct((bsz, n_heads, seq, qk), BF16),
        compiler_params=_params(("arbitrary", "arbitrary")),
        name="mla_q",
    )(x2, mod, g_mix.reshape(1, d), w_dq.astype(BF16), g_cq.reshape(1, q_rank), wu, cos, sin)


ATTN_BLOCK = 512


def _flash_kernel(qi_ref, ki_ref, q_ref, k_ref, v_ref, o_ref, m_ref, l_ref, acc_ref):
    p = pl.program_id(2)
    qi = qi_ref[p]
    ki = ki_ref[p]
    tq = q_ref.shape[2]
    tk = k_ref.shape[2]

    @pl.when(ki == 0)
    def _():
        m_ref[...] = jnp.full_like(m_ref, NEG_BIG)
        l_ref[...] = jnp.zeros_like(l_ref)
        acc_ref[...] = jnp.zeros_like(acc_ref)

    s = _dot_nt(q_ref[0, 0], k_ref[0, 0])

    def update(s):
        m_prev = m_ref[...]
        m_new = jnp.maximum(m_prev, jnp.max(s, axis=-1, keepdims=True))
        alpha = jnp.exp(m_prev - m_new)
        pr = jnp.exp(s - m_new)
        l_ref[...] = alpha * l_ref[...] + jnp.sum(pr, axis=-1, keepdims=True)
        acc_ref[...] = alpha * acc_ref[...] + _dot(pr.astype(BF16), v_ref[0, 0])
        m_ref[...] = m_new

    @pl.when(ki < qi)
    def _():
        update(s)

    @pl.when(ki == qi)
    def _():
        row = lax.broadcasted_iota(I32, (tq, tk), 0)
        col = lax.broadcasted_iota(I32, (tq, tk), 1)
        update(jnp.where(col <= row, s, NEG_BIG))
        o_ref[0] = (acc_ref[...] / l_ref[...]).astype(BF16)


def _flash_attn(q, k, v):
    bsz, n_heads, seq, qk = q.shape
    dv = v.shape[3]
    t = ATTN_BLOCK
    nq = seq // t
    pairs = [(a, b) for a in range(nq) for b in range(a + 1)]
    qi = jnp.asarray([a for a, _ in pairs], I32)
    ki = jnp.asarray([b for _, b in pairs], I32)
    grid_spec = pltpu.PrefetchScalarGridSpec(
        num_scalar_prefetch=2,
        grid=(bsz, n_heads, len(pairs)),
        in_specs=[
            pl.BlockSpec((1, 1, t, qk), lambda b, h, p, qi, ki: (b, h, qi[p], 0)),
            pl.BlockSpec((1, 1, t, qk), lambda b, h, p, qi, ki: (b, h, ki[p], 0)),
            pl.BlockSpec((1, 1, t, dv), lambda b, h, p, qi, ki: (b, h, ki[p], 0)),
        ],
        out_specs=pl.BlockSpec((1, t, dv), lambda b, h, p, qi, ki: (b, qi[p], h)),
        scratch_shapes=[
            pltpu.VMEM((t, 1), F32), pltpu.VMEM((t, 1), F32), pltpu.VMEM((t, dv), F32),
        ],
    )
    return pl.pallas_call(
        _flash_kernel,
        grid_spec=grid_spec,
        out_shape=jax.ShapeDtypeStruct((bsz, seq, n_heads * dv), BF16),
        compiler_params=_params(("arbitrary", "arbitrary", "arbitrary")),
        name="flash_attn",
    )(qi, ki, q, k, v)


def _attn_out_kernel(x_ref, mod_ref, o_ref, w_ref, y_ref):
    y_ref[...] = x_ref[...] + mod_ref[0][2:3] * _dot(o_ref[...], w_ref[...])


def _attn_out(x2, mod, o2, w_o_bf, seq):
    n, d = x2.shape
    tm = 512
    tiles_per_b = seq // tm
    return pl.pallas_call(
        _attn_out_kernel,
        grid=(n // tm,),
        in_specs=[
            pl.BlockSpec((tm, d), lambda i: (i, 0)),
            pl.BlockSpec((1, 6, d), lambda i: (i // tiles_per_b, 0, 0)),
            pl.BlockSpec((tm, o2.shape[1]), lambda i: (i, 0)),
            pl.BlockSpec(w_o_bf.shape, lambda i: (0, 0)),
        ],
        out_specs=pl.BlockSpec((tm, d), lambda i: (i, 0)),
        out_shape=jax.ShapeDtypeStruct((n, d), F32),
        compiler_params=_params(("arbitrary",)),
        name="attn_out",
    )(x2, mod, o2, w_o_bf)


def _final_kernel(x_ref, g_ref, o_ref):
    o_ref[...] = _rms(x_ref[...], g_ref[...])


def _final_norm(x2, g):
    n, d = x2.shape
    tm = 1024
    return pl.pallas_call(
        _final_kernel,
        grid=(n // tm,),
        in_specs=[pl.BlockSpec((tm, d), lambda i: (i, 0)), pl.BlockSpec((1, d), lambda i: (0, 0))],
        out_specs=pl.BlockSpec((tm, d), lambda i: (i, 0)),
        out_shape=jax.ShapeDtypeStruct((n, d), F32),
        compiler_params=_params(("arbitrary",)),
        name="final_norm",
    )(x2, g.reshape(1, d))


def kernel(x, c, positions, g_mix, g_ffn, w_ada, b_ada, w_in_a, lb_logits, g_out_a, w_out_a, g_kv, w_ada_kv, b_ada_kv, w_dkv, g_ckv, w_ukv, w_dq, g_cq, w_uq, w_o_b, w_router, b_router, w_gu, b_gu, w_dn, b_dn, g_final):
    bsz, seq, d = x.shape
    depth = g_mix.shape[0]
    n_a = w_in_a.shape[0]
    n = bsz * seq
    x2 = x.reshape(n, d)

    mods = _ada_mod(c, w_ada, b_ada)
    mods = mods.reshape(depth, bsz, 6, d)
    mod_kv = _ada_mod(c, w_ada_kv[None], b_ada_kv[None]).reshape(bsz, 2, d)

    shared = None
    cos = sin = None
    for l in range(depth):
        mod = mods[l]
        if l < n_a:
            q, k, v, lf, gate = _hgrn_in(x2, mod, g_mix[l], lb_logits, w_in_a[l].astype(BF16), l, seq)
            x2 = _hgrn_rec(x2, mod, q, k, v, lf, gate, g_out_a[l], w_out_a[l].astype(BF16), bsz, seq)
        else:
            j = l - n_a
            qh = _mla_q(x2, mod, g_mix[l], w_dq[j], g_cq[j], w_uq[j], cos, sin, bsz, seq)
            o2 = _flash_attn(qh, *shared)
            x2 = _attn_out(x2, mod, o2.reshape(n, -1), w_o_b[j].astype(BF16), seq)
        x2 = _moe_layer(x2, mod, g_ffn[l], w_router[l], b_router[l],
                        w_gu[l].astype(BF16), b_gu[l], w_dn[l].astype(BF16), b_dn[l], seq)
        if l == n_a - 1:
            cos, sin = _rope_table(positions)
            shared = _kv_shared(x2, mod_kv, g_kv, w_dkv, g_ckv, w_ukv, cos, sin, bsz, seq)
    return _final_norm(x2, g_final).reshape(bsz, seq, d)
```

```python
import functools
import math

import numpy as np
import jax
import jax.numpy as jnp
from jax import lax
from jax.experimental import pallas as pl
from jax.experimental.pallas import tpu as pltpu

F32 = jnp.float32
BF16 = jnp.bfloat16
I32 = jnp.int32

LANES = 128
SUBLANES = 8
VMEM_LIMIT = 48 * 1024 * 1024
EXPERT_VMEM_LIMIT = 56 * 1024 * 1024

HG_HEAD_DIM = 128
HG_CHUNK = 64
FORGET_FLOOR = 1e-30
MLA_NOPE = 128
MLA_ROPE = 64
MLA_V = 128
MLA_KV_RANK = 256
ROPE_THETA = 10000.0
N_EXPERTS = 32
TOP_K = 4
SWIGLU_LIMIT = 7.0
SWIGLU_ALPHA = 1.702
MOE_BLOCK = 256
NORM_EPS = 1e-6
NEG_BIG = -1e30

N_LEVELS = int(math.log2(HG_CHUNK))


def _params(sem, vmem=VMEM_LIMIT):
    return pltpu.CompilerParams(dimension_semantics=sem, vmem_limit_bytes=vmem)


def _dot(a, b):
    return jnp.dot(a, b, preferred_element_type=F32)


def _dot_nt(a, b):
    return lax.dot_general(a, b, (((1,), (1,)), ((), ())), preferred_element_type=F32)


def _split2(a):
    hi = a.astype(BF16)
    lo = (a - hi.astype(F32)).astype(BF16)
    return hi, lo


def _dot3(a, b):
    a_hi, a_lo = _split2(a)
    b_hi, b_lo = _split2(b)
    return _dot(a_hi, b_hi) + _dot(a_lo, b_hi) + _dot(a_hi, b_lo)


def _sigmoid(x):
    return 1.0 / (1.0 + jnp.exp(-x))


def _rms(x, gain):
    ms = jnp.mean(x * x, axis=-1, keepdims=True)
    return x * lax.rsqrt(ms + NORM_EPS) * gain


def _norm_mod(x, gain, shift, scale):
    return _rms(x, gain) * (1.0 + scale) + shift


def _ada_kernel(c_ref, w_ref, b_ref, o_ref):
    c = c_ref[...]
    c_act = c * _sigmoid(c)
    o_ref[0] = _dot3(c_act, w_ref[0]) + b_ref[0]


def _ada_mod(c, w, b):
    n_l, d, m = w.shape
    bsz = c.shape[0]
    tn = 1024
    return pl.pallas_call(
        _ada_kernel,
        grid=(n_l, m // tn),
        in_specs=[
            pl.BlockSpec((bsz, d), lambda l, j: (0, 0)),
            pl.BlockSpec((1, d, tn), lambda l, j: (l, 0, j)),
            pl.BlockSpec((1, 1, tn), lambda l, j: (l, 0, j)),
        ],
        out_specs=pl.BlockSpec((1, bsz, tn), lambda l, j: (l, 0, j)),
        out_shape=jax.ShapeDtypeStruct((n_l, bsz, m), F32),
        compiler_params=_params(("arbitrary", "arbitrary")),
        name="ada_mod",
    )(c, w, b.reshape(n_l, 1, m))


def _hgrn_in_kernel(layer, n_a, x_ref, mod_ref, g_ref, lbl_ref, w_ref,
                    q_ref, k_ref, v_ref, lf_ref, gate_ref):
    d = x_ref.shape[1]
    mod = mod_ref[0]
    h = _norm_mod(x_ref[...], g_ref[...], mod[0:1], mod[1:2]).astype(BF16)
    rows = [lbl_ref[i:i + 1, :] for i in range(n_a)]
    mx = functools.reduce(jnp.maximum, rows)
    ex = [jnp.exp(r - mx) for r in rows]
    den = functools.reduce(lambda a, b: a + b, ex)
    lb = (functools.reduce(lambda a, b: a + b, ex[:layer + 1]) - ex[0]) / den

    yq = _dot(h, w_ref[:, 0:d])
    q_ref[...] = (yq * _sigmoid(yq)).astype(BF16)
    yf = _dot(h, w_ref[:, d:2 * d])
    sg = _sigmoid(yf)
    forget = lb + (1.0 - lb) * sg
    lf_ref[...] = jnp.log(jnp.maximum(forget, FORGET_FLOOR))
    k_ref[...] = ((1.0 - lb) * (1.0 - sg)).astype(BF16)
    v_ref[...] = _dot(h, w_ref[:, 2 * d:3 * d]).astype(BF16)
    yg = _dot(h, w_ref[:, 3 * d:4 * d])
    gate_ref[...] = (yg * _sigmoid(yg)).astype(BF16)


def _hgrn_in(x2, mod, g_mix, lb_logits, w_in_bf, layer, seq):
    n, d = x2.shape
    tm = 512
    n_a = lb_logits.shape[0]
    tiles_per_b = seq // tm
    row = lambda i: (i, 0)
    return pl.pallas_call(
        functools.partial(_hgrn_in_kernel, layer, n_a),
        grid=(n // tm,),
        in_specs=[
            pl.BlockSpec((tm, d), row),
            pl.BlockSpec((1, 6, d), lambda i: (i // tiles_per_b, 0, 0)),
            pl.BlockSpec((1, d), lambda i: (0, 0)),
            pl.BlockSpec((n_a, d), lambda i: (0, 0)),
            pl.BlockSpec((d, 4 * d), lambda i: (0, 0)),
        ],
        out_specs=[pl.BlockSpec((tm, d), row)] * 5,
        out_shape=[jax.ShapeDtypeStruct((n, d), BF16)] * 3
        + [jax.ShapeDtypeStruct((n, d), F32), jax.ShapeDtypeStruct((n, d), BF16)],
        compiler_params=_params(("arbitrary",)),
        name="hgrn_in",
    )(x2, mod, g_mix.reshape(1, d), lb_logits, w_in_bf)


def _mid_rows(cum, half):
    c, d = cum.shape
    blk = 2 * half
    if blk >= SUBLANES:
        parts = [jnp.broadcast_to(cum[b * blk + half - 1:b * blk + half, :], (blk, d))
                 for b in range(c // blk)]
        return jnp.concatenate(parts, axis=0)
    sub = lax.broadcasted_iota(I32, (SUBLANES, 1), 0)
    parts = []
    for g in range(c // SUBLANES):
        out = None
        for j in range(SUBLANES // blk):
            r = g * SUBLANES + j * blk + half - 1
            cand = jnp.broadcast_to(cum[r:r + 1, :], (SUBLANES, d))
            out = cand if out is None else jnp.where(sub >= j * blk, cand, out)
        parts.append(out)
    return jnp.concatenate(parts, axis=0)


def _hgrn_rec_kernel(n_heads, x_ref, mod_ref, q_ref, k_ref, v_ref, lf_ref, gate_ref,
                     tri_ref, gout_ref, wout_ref, o_ref, state_ref, obuf_ref):
    c = HG_CHUNK
    hd = HG_HEAD_DIM
    ts = x_ref.shape[0]

    @pl.when(pl.program_id(1) == 0)
    def _():
        state_ref[...] = jnp.zeros_like(state_ref)

    t_idx = lax.broadcasted_iota(I32, (c, c), 0)
    s_idx = lax.broadcasted_iota(I32, (c, c), 1)
    r_idx = lax.broadcasted_iota(I32, (c, 1), 0)
    pair_masks = []
    right_rows = []
    for lvl in range(N_LEVELS):
        pair_masks.append(jnp.logical_and(t_idx > s_idx, ((t_idx ^ s_idx) >> lvl) == 1))
        right_rows.append(((r_idx >> lvl) & 1) == 1)
    diag_mask = t_idx == s_idx
    tri = tri_ref[...]
    gout = gout_ref[...]

    def chunk_body(ci, carry):
        r0 = pl.multiple_of(ci * c, c)
        rows = pl.ds(r0, c)
        lf = lf_ref[rows, :]
        qa = q_ref[rows, :].astype(F32)
        ka = k_ref[rows, :].astype(F32)
        va = v_ref[rows, :].astype(F32)
        l_hi = lf.astype(BF16)
        rem = lf - l_hi.astype(F32)
        l_mid = rem.astype(BF16)
        l_lo = (rem - l_mid.astype(F32)).astype(BF16)
        cum = _dot(tri, l_hi) + _dot(tri, l_mid) + _dot(tri, l_lo)
        e_cum = jnp.exp(cum)
        e_tail = jnp.exp(jnp.broadcast_to(cum[c - 1:c, :], cum.shape) - cum)
        q_in = (qa * e_cum).astype(BF16)
        k_out = (ka * e_tail).astype(BF16)
        vb = va.astype(BF16)
        qk_diag = qa * ka
        qk_sub = qa * jnp.exp(lf) * pltpu.roll(ka, 1, 0)
        xs = []
        for lvl in range(1, N_LEVELS):
            mid = _mid_rows(cum, 1 << lvl)
            dlt = cum - mid
            e = jnp.exp(jnp.where(right_rows[lvl], dlt, -dlt))
            xs.append((jnp.where(right_rows[lvl], qa, ka) * e).astype(BF16))
        for h in range(n_heads):
            cols = slice(h * hd, (h + 1) * hd)
            d_col = jnp.sum(qk_diag[:, cols], axis=-1, keepdims=True)
            s_col = jnp.sum(qk_sub[:, cols], axis=-1, keepdims=True)
            sc = jnp.where(diag_mask, d_col, jnp.where(pair_masks[0], s_col, 0.0))
            for lvl in range(1, N_LEVELS):
                xl = xs[lvl - 1][:, cols]
                sc = jnp.where(pair_masks[lvl], _dot_nt(xl, xl), sc)
            st = state_ref[h]
            o = _dot_nt(q_in[:, cols], st.astype(BF16)) + _dot(sc.astype(BF16), vb[:, cols])
            upd = _dot(va[:, cols].T.astype(BF16), k_out[:, cols])
            state_ref[h] = st * e_cum[c - 1:c, cols] + upd
            on = _rms(o, gout) * gate_ref[rows, cols].astype(F32)
            obuf_ref[rows, cols] = on.astype(BF16)
        return carry

    lax.fori_loop(0, ts // c, chunk_body, 0)
    y = _dot(obuf_ref[...], wout_ref[...])
    o_ref[...] = x_ref[...] + mod_ref[0][2:3] * y


def _hgrn_rec(x2, mod, q, k, v, lf, gate, g_out, w_out_bf, bsz, seq):
    n, d = x2.shape
    ts = 512
    n_heads = d // HG_HEAD_DIM
    tiles = seq // ts
    row = lambda b, j: (b * tiles + j, 0)
    const = lambda b, j: (0, 0)
    tri = jnp.asarray(np.tril(np.ones((HG_CHUNK, HG_CHUNK), np.float32)), BF16)
    return pl.pallas_call(
        functools.partial(_hgrn_rec_kernel, n_heads),
        grid=(bsz, tiles),
        in_specs=[
            pl.BlockSpec((ts, d), row),
            pl.BlockSpec((1, 6, d), lambda b, j: (b, 0, 0)),
            pl.BlockSpec((ts, d), row), pl.BlockSpec((ts, d), row), pl.BlockSpec((ts, d), row),
            pl.BlockSpec((ts, d), row), pl.BlockSpec((ts, d), row),
            pl.BlockSpec(tri.shape, const),
            pl.BlockSpec((1, HG_HEAD_DIM), const),
            pl.BlockSpec((d, d), const),
        ],
        out_specs=pl.BlockSpec((ts, d), row),
        out_shape=jax.ShapeDtypeStruct((n, d), F32),
        scratch_shapes=[
            pltpu.VMEM((n_heads, HG_HEAD_DIM, HG_HEAD_DIM), F32),
            pltpu.VMEM((ts, d), BF16),
        ],
        compiler_params=_params(("arbitrary", "arbitrary")),
        name="hgrn_rec",
    )(x2, mod, q, k, v, lf, gate, tri, g_out.reshape(1, HG_HEAD_DIM), w_out_bf)


def _store_token_tiles(ref, val):
    tm = val.shape[0]
    for cidx in range(val.shape[1] // LANES):
        ref[pl.ds(cidx, tm, stride=SUBLANES), :] = val[:, cidx * LANES:(cidx + 1) * LANES]


def _load_token_tiles(ref, tm):
    n_c = ref.shape[0] // tm
    return jnp.concatenate([ref[pl.ds(cidx, tm, stride=SUBLANES), :] for cidx in range(n_c)], axis=-1)


def _router_kernel(x_ref, mod_ref, g_ref, wr_ref, br_ref, h_ref, idx_ref, wt_ref, cnt_ref):
    tm = x_ref.shape[0]
    mod = mod_ref[0]
    h = _norm_mod(x_ref[...], g_ref[...], mod[3:4], mod[4:5])
    _store_token_tiles(h_ref, h)
    logits = _dot3(h, wr_ref[...]) + br_ref[...]
    lane = lax.broadcasted_iota(I32, (tm, LANES), 1)
    lane_f = lane.astype(F32)
    idx_acc = jnp.zeros((tm, LANES), F32)
    val_acc = jnp.zeros((tm, LANES), F32)
    cnt = jnp.zeros((tm, LANES), F32)
    work = logits
    for kk in range(TOP_K):
        m = jnp.max(work, axis=-1, keepdims=True)
        sel = jnp.min(jnp.where(work == m, lane_f, float(LANES)), axis=-1, keepdims=True)
        hit = lane_f == sel
        work = jnp.where(hit, -jnp.inf, work)
        cnt = cnt + jnp.where(hit, 1.0, 0.0)
        idx_acc = jnp.where(lane == kk, sel, idx_acc)
        val_acc = jnp.where(lane == kk, m, val_acc)
    top = jnp.max(jnp.where(lane < TOP_K, val_acc, -jnp.inf), axis=-1, keepdims=True)
    ex = jnp.where(lane < TOP_K, jnp.exp(val_acc - top), 0.0)
    wts = ex / jnp.sum(ex, axis=-1, keepdims=True)
    idx_ref[...] = idx_acc[:, :TOP_K].astype(I32)
    wt_ref[...] = wts[:, :TOP_K]

    @pl.when(pl.program_id(0) == 0)
    def _():
        cnt_ref[...] = jnp.zeros_like(cnt_ref)

    cnt_ref[...] += jnp.sum(cnt, axis=0, keepdims=True)


def _moe_router(x2, mod, g_ffn, w_router, b_router, seq):
    n, d = x2.shape
    tm = 512
    tiles_per_b = seq // tm
    wr = jnp.zeros((d, LANES), F32).at[:, :N_EXPERTS].set(w_router)
    br = jnp.full((1, LANES), NEG_BIG, F32).at[0, :N_EXPERTS].set(b_router)
    return pl.pallas_call(
        _router_kernel,
        grid=(n // tm,),
        in_specs=[
            pl.BlockSpec((tm, d), lambda i: (i, 0)),
            pl.BlockSpec((1, 6, d), lambda i: (i // tiles_per_b, 0, 0)),
            pl.BlockSpec((1, d), lambda i: (0, 0)),
            pl.BlockSpec((d, LANES), lambda i: (0, 0)),
            pl.BlockSpec((1, LANES), lambda i: (0, 0)),
        ],
        out_specs=[
            pl.BlockSpec((tm * SUBLANES, LANES), lambda i: (i, 0)),
            pl.BlockSpec((tm, TOP_K), lambda i: (i, 0)),
            pl.BlockSpec((tm, TOP_K), lambda i: (i, 0)),
            pl.BlockSpec((1, LANES), lambda i: (0, 0)),
        ],
        out_shape=[
            jax.ShapeDtypeStruct((n * SUBLANES, LANES), F32),
            jax.ShapeDtypeStruct((n, TOP_K), I32),
            jax.ShapeDtypeStruct((n, TOP_K), F32),
            jax.ShapeDtypeStruct((1, LANES), F32),
        ],
        compiler_params=_params(("arbitrary",)),
        name="moe_router",
    )(x2, mod, g_ffn.reshape(1, d), wr, br)


def _dest_kernel(idx_ref, start_ref, tri_ref, dest_ref, carry_ref):
    tm = idx_ref.shape[0]

    @pl.when(pl.program_id(0) == 0)
    def _():
        carry_ref[...] = jnp.zeros_like(carry_ref)

    lane = lax.broadcasted_iota(I32, (tm, LANES), 1)
    idx = idx_ref[...]
    hits = [lane == idx[:, kk:kk + 1] for kk in range(TOP_K)]
    cnt = functools.reduce(lambda a, b: a + b, [jnp.where(hh, 1.0, 0.0) for hh in hits])
    before = _dot(tri_ref[...], cnt.astype(BF16))
    pos = before + carry_ref[...] + start_ref[...]
    acc = jnp.zeros((tm, LANES), F32)
    for kk in range(TOP_K):
        dk = jnp.sum(jnp.where(hits[kk], pos, 0.0), axis=-1, keepdims=True)
        acc = jnp.where(lane == kk, dk, acc)
    dest_ref[...] = acc[:, :TOP_K].astype(I32)
    carry_ref[...] += jnp.sum(cnt, axis=0, keepdims=True)


def _moe_dest(idx, pad_start):
    n = idx.shape[0]
    tm = 512
    tri = jnp.asarray(np.tril(np.ones((tm, tm), np.float32), -1), BF16)
    return pl.pallas_call(
        _dest_kernel,
        grid=(n // tm,),
        in_specs=[
            pl.BlockSpec((tm, TOP_K), lambda i: (i, 0)),
            pl.BlockSpec((1, LANES), lambda i: (0, 0)),
            pl.BlockSpec((tm, tm), lambda i: (0, 0)),
        ],
        out_specs=pl.BlockSpec((tm, TOP_K), lambda i: (i, 0)),
        out_shape=jax.ShapeDtypeStruct((n, TOP_K), I32),
        scratch_shapes=[pltpu.VMEM((1, LANES), F32)],
        compiler_params=_params(("arbitrary",)),
        name="moe_dest",
    )(idx, pad_start, tri)


DISPATCH_TOKENS = 256


def _token_copy(src, src_tok, dst, dst_tok, sem):
    s0 = pl.multiple_of(src_tok * SUBLANES, SUBLANES)
    d0 = pl.multiple_of(dst_tok * SUBLANES, SUBLANES)
    return pltpu.make_async_copy(src.at[pl.ds(s0, SUBLANES), :], dst.at[pl.ds(d0, SUBLANES), :], sem)


def _dispatch_kernel(gend_ref, gpad_ref, dest_ref, h_ref, xs_ref, zero_ref, sem, zsem):
    t = DISPATCH_TOKENS
    blk_rows = MOE_BLOCK * SUBLANES

    @pl.when(pl.program_id(0) == 0)
    def _():
        zero_ref[...] = jnp.zeros_like(zero_ref)

        def zero_copy(e):
            start = pl.multiple_of((gend_ref[e] - MOE_BLOCK) * SUBLANES, SUBLANES)
            return pltpu.make_async_copy(zero_ref, xs_ref.at[pl.ds(start, blk_rows), :], zsem)

        n_blocks = xs_ref.shape[0] // blk_rows
        first_free = gend_ref[N_EXPERTS - 1] // MOE_BLOCK

        def free_copy(j):
            start = pl.multiple_of((first_free + j) * blk_rows, blk_rows)
            return pltpu.make_async_copy(zero_ref, xs_ref.at[pl.ds(start, blk_rows), :], zsem)

        for e in range(N_EXPERTS):
            @pl.when(gpad_ref[e] > 0)
            def _():
                zero_copy(e).start()

            @pl.when(first_free + e < n_blocks)
            def _():
                free_copy(e).start()
        for e in range(N_EXPERTS):
            @pl.when(gpad_ref[e] > 0)
            def _():
                zero_copy(e).wait()

            @pl.when(first_free + e < n_blocks)
            def _():
                free_copy(e).wait()

    def start(tok, carry):
        for kk in range(TOP_K):
            _token_copy(h_ref, tok, xs_ref, dest_ref[tok * TOP_K + kk], sem).start()
        return carry

    lax.fori_loop(0, t, start, 0)
    for _ in range(TOP_K):
        pltpu.make_async_copy(h_ref, xs_ref.at[pl.ds(0, t * SUBLANES), :], sem).wait()


def _moe_dispatch(h_tt, dest, pad_end, padded, n_rows):
    n = dest.shape[0]
    t = DISPATCH_TOKENS
    grid_spec = pltpu.PrefetchScalarGridSpec(
        num_scalar_prefetch=2,
        grid=(n // t,),
        in_specs=[
            pl.BlockSpec((t * TOP_K,), lambda i, ge, gp: (i,), memory_space=pltpu.SMEM),
            pl.BlockSpec((t * SUBLANES, LANES), lambda i, ge, gp: (i, 0)),
        ],
        out_specs=pl.BlockSpec(memory_space=pl.ANY),
        scratch_shapes=[
            pltpu.VMEM((MOE_BLOCK * SUBLANES, LANES), F32),
            pltpu.SemaphoreType.DMA(()),
            pltpu.SemaphoreType.DMA(()),
        ],
    )
    return pl.pallas_call(
        _dispatch_kernel,
        grid_spec=grid_spec,
        out_shape=jax.ShapeDtypeStruct((n_rows * SUBLANES, LANES), F32),
        compiler_params=_params(("arbitrary",)),
        name="moe_dispatch",
    )(pad_end, padded, dest.reshape(n * TOP_K), h_tt)


def _experts_kernel(be_ref, na_ref, xs_ref, wgu_ref, bgu_ref, wdn_ref, bdn_ref, ys_ref, wgu_bf, wdn_bf):
    i = pl.program_id(0)
    tm = MOE_BLOCK
    ff = wdn_ref.shape[1]

    @pl.when((i == 0) | (be_ref[i] != be_ref[jnp.maximum(i - 1, 0)]))
    def _():
        wgu_bf[...] = wgu_ref[0].astype(BF16)
        wdn_bf[...] = wdn_ref[0].astype(BF16)

    @pl.when(i < na_ref[0])
    def _():
        xb = _load_token_tiles(xs_ref, tm).astype(BF16)
        gu = _dot(xb, wgu_bf[...]) + bgu_ref[0]
        gate = jnp.minimum(gu[:, :ff], SWIGLU_LIMIT)
        up = jnp.clip(gu[:, ff:], -SWIGLU_LIMIT, SWIGLU_LIMIT)
        glu = gate * _sigmoid(SWIGLU_ALPHA * gate)
        y = _dot(((up + 1.0) * glu).astype(BF16), wdn_bf[...]) + bdn_ref[0]
        _store_token_tiles(ys_ref, y)

    @pl.when(i >= na_ref[0])
    def _():
        ys_ref[...] = jnp.zeros_like(ys_ref)


def _moe_experts(xs, block_e, n_active, w_gu, b_gu, w_dn, b_dn):
    n_e, d, ff2 = w_gu.shape
    ff = ff2 // 2
    n_blocks = block_e.shape[0]
    tm = MOE_BLOCK
    grid_spec = pltpu.PrefetchScalarGridSpec(
        num_scalar_prefetch=2,
        grid=(n_blocks,),
        in_specs=[
            pl.BlockSpec((tm * SUBLANES, LANES), lambda i, be, na: (jnp.minimum(i, na[0] - 1), 0)),
            pl.BlockSpec((1, d, ff2), lambda i, be, na: (be[i], 0, 0)),
            pl.BlockSpec((1, 1, ff2), lambda i, be, na: (be[i], 0, 0)),
            pl.BlockSpec((1, ff, d), lambda i, be, na: (be[i], 0, 0)),
            pl.BlockSpec((1, 1, d), lambda i, be, na: (be[i], 0, 0)),
        ],
        out_specs=pl.BlockSpec((tm * SUBLANES, LANES), lambda i, be, na: (i, 0)),
        scratch_shapes=[pltpu.VMEM((d, ff2), BF16), pltpu.VMEM((ff, d), BF16)],
    )
    return pl.pallas_call(
        _experts_kernel,
        grid_spec=grid_spec,
        out_shape=jax.ShapeDtypeStruct(xs.shape, F32),
        compiler_params=_params(("arbitrary",), EXPERT_VMEM_LIMIT),
        name="moe_experts",
    )(block_e, n_active, xs, w_gu, b_gu.reshape(n_e, 1, ff2), w_dn, b_dn.reshape(n_e, 1, d))


def _combine_kernel(dest_ref, x_ref, mod_ref, wt_ref, ys_ref, o_ref, buf_ref, sem):
    t = DISPATCH_TOKENS
    slot_rows = t * SUBLANES

    def start(tok, carry):
        for kk in range(TOP_K):
            _token_copy(ys_ref, dest_ref[tok * TOP_K + kk], buf_ref, kk * t + tok, sem).start()
        return carry

    lax.fori_loop(0, t, start, 0)
    pltpu.make_async_copy(ys_ref.at[pl.ds(0, TOP_K * slot_rows), :], buf_ref, sem).wait()

    wt = wt_ref[...]
    y = jnp.zeros(x_ref.shape, F32)
    for kk in range(TOP_K):
        y = y + _load_token_tiles(buf_ref.at[pl.ds(kk * slot_rows, slot_rows), :], t) * wt[:, kk:kk + 1]
    o_ref[...] = x_ref[...] + mod_ref[0][5:6] * y


def _moe_combine(x2, mod, wts, dest, ys, seq):
    n, d = x2.shape
    t = DISPATCH_TOKENS
    tiles_per_b = seq // t
    return pl.pallas_call(
        _combine_kernel,
        grid=(n // t,),
        in_specs=[
            pl.BlockSpec((t * TOP_K,), lambda i: (i,), memory_space=pltpu.SMEM),
            pl.BlockSpec((t, d), lambda i: (i, 0)),
            pl.BlockSpec((1, 6, d), lambda i: (i // tiles_per_b, 0, 0)),
            pl.BlockSpec((t, TOP_K), lambda i: (i, 0)),
            pl.BlockSpec(memory_space=pl.ANY),
        ],
        out_specs=pl.BlockSpec((t, d), lambda i: (i, 0)),
        out_shape=jax.ShapeDtypeStruct((n, d), F32),
        scratch_shapes=[
            pltpu.VMEM((TOP_K * t * SUBLANES, LANES), F32),
            pltpu.SemaphoreType.DMA(()),
        ],
        compiler_params=_params(("arbitrary",)),
        name="moe_combine",
    )(dest.reshape(n * TOP_K), x2, mod, wts, ys)


def _moe_layer(x2, mod, g_ffn, w_router, b_router, w_gu, b_gu, w_dn, b_dn, seq):
    n = x2.shape[0]
    h_tt, idx, wts, counts = _moe_router(x2, mod, g_ffn, w_router, b_router, seq)
    counts = counts[0, :N_EXPERTS].astype(I32)
    padded = (counts + MOE_BLOCK - 1) // MOE_BLOCK * MOE_BLOCK
    pad_end = jnp.cumsum(padded)
    pad_start = pad_end - padded
    n_rows = n * TOP_K + N_EXPERTS * MOE_BLOCK
    n_blocks = n_rows // MOE_BLOCK
    blk_start = jnp.arange(n_blocks, dtype=I32) * MOE_BLOCK
    block_e = jnp.minimum(jnp.sum((pad_end[None, :] <= blk_start[:, None]).astype(I32), axis=1),
                          N_EXPERTS - 1)
    n_active = (pad_end[-1:] // MOE_BLOCK).astype(I32)
    start_vec = jnp.zeros((1, LANES), F32).at[0, :N_EXPERTS].set(pad_start.astype(F32))
    dest = _moe_dest(idx, start_vec)
    xs = _moe_dispatch(h_tt, dest, pad_end.astype(I32), padded.astype(I32), n_rows)
    ys = _moe_experts(xs, block_e, n_active, w_gu, b_gu, w_dn, b_dn)
    return _moe_combine(x2, mod, wts, dest, ys, seq)


def _rope_kernel(pos_ref, inv_ref, cos_ref, sin_ref):
    ang = pos_ref[...] * inv_ref[...]
    cos_ref[...] = jnp.cos(ang)
    sin_ref[...] = jnp.sin(ang)


def _rope_table(positions):
    half = MLA_ROPE // 2
    n = positions.size
    rep = LANES // half
    inv = np.power(np.float32(ROPE_THETA),
                   -np.arange(half, dtype=np.float32) * np.float32(2.0 / MLA_ROPE)).astype(np.float32)
    inv_row = jnp.asarray(np.tile(inv, rep).reshape(1, LANES))
    pos = jnp.repeat(positions.reshape(n).astype(F32), half).reshape(n // rep, LANES)
    rows = n // rep
    tr = min(rows, 2048)
    cos, sin = pl.pallas_call(
        _rope_kernel,
        grid=(rows // tr,),
        in_specs=[pl.BlockSpec((tr, LANES), lambda i: (i, 0)), pl.BlockSpec((1, LANES), lambda i: (0, 0))],
        out_specs=[pl.BlockSpec((tr, LANES), lambda i: (i, 0))] * 2,
        out_shape=[jax.ShapeDtypeStruct((rows, LANES), F32)] * 2,
        compiler_params=_params(("arbitrary",)),
        name="rope_table",
    )(pos, inv_row)
    return cos.reshape(n, half), sin.reshape(n, half)


def _rope_halves(t1, t2, cos, sin):
    return t1 * cos - t2 * sin, t2 * cos + t1 * sin


def _kv_kernel(n_heads, x_ref, mod_ref, g_ref, wd_ref, gc_ref, wu_ref, cos_ref, sin_ref, k_ref, v_ref):
    half = MLA_ROPE // 2
    mod = mod_ref[0]
    hk = _norm_mod(x_ref[...], g_ref[...], mod[0:1], mod[1:2]).astype(BF16)
    ckr = _dot(hk, wd_ref[...])
    c_kv = _rms(ckr[:, :MLA_KV_RANK], gc_ref[...]).astype(BF16)
    r1, r2 = _rope_halves(ckr[:, MLA_KV_RANK:MLA_KV_RANK + half], ckr[:, MLA_KV_RANK + half:],
                          cos_ref[...], sin_ref[...])
    kv = _dot(c_kv, wu_ref[...])
    nv = n_heads * MLA_NOPE
    for h in range(n_heads):
        k_ref[0, h] = jnp.concatenate(
            [kv[:, h * MLA_NOPE:(h + 1) * MLA_NOPE], r1, r2], axis=-1).astype(BF16)
        v_ref[0, h] = kv[:, nv + h * MLA_V:nv + (h + 1) * MLA_V].astype(BF16)


def _kv_shared(x2, mod_kv, g_kv, w_dkv, g_ckv, w_ukv, cos, sin, bsz, seq):
    n, d = x2.shape
    n_heads = w_ukv.shape[1] // (MLA_NOPE + MLA_V)
    half = MLA_ROPE // 2
    tm = 512
    tiles = seq // tm
    w3 = w_ukv.reshape(MLA_KV_RANK, n_heads, MLA_NOPE + MLA_V)
    wu = jnp.concatenate([w3[:, :, :MLA_NOPE].reshape(MLA_KV_RANK, -1),
                          w3[:, :, MLA_NOPE:].reshape(MLA_KV_RANK, -1)], axis=1).astype(BF16)
    row = lambda b, j: (b * tiles + j, 0)
    const = lambda b, j: (0, 0)
    qk = MLA_NOPE + MLA_ROPE
    return pl.pallas_call(
        functools.partial(_kv_kernel, n_heads),
        grid=(bsz, tiles),
        in_specs=[
            pl.BlockSpec((tm, d), row),
            pl.BlockSpec((1, 2, d), lambda b, j: (b, 0, 0)),
            pl.BlockSpec((1, d), const),
            pl.BlockSpec(w_dkv.shape, const),
            pl.BlockSpec((1, MLA_KV_RANK), const),
            pl.BlockSpec(wu.shape, const),
            pl.BlockSpec((tm, half), row),
            pl.BlockSpec((tm, half), row),
        ],
        out_specs=[
            pl.BlockSpec((1, n_heads, tm, qk), lambda b, j: (b, 0, j, 0)),
            pl.BlockSpec((1, n_heads, tm, MLA_V), lambda b, j: (b, 0, j, 0)),
        ],
        out_shape=[
            jax.ShapeDtypeStruct((bsz, n_heads, seq, qk), BF16),
            jax.ShapeDtypeStruct((bsz, n_heads, seq, MLA_V), BF16),
        ],
        compiler_params=_params(("arbitrary", "arbitrary")),
        name="kv_shared",
    )(x2, mod_kv, g_kv.reshape(1, d), w_dkv.astype(BF16), g_ckv.reshape(1, MLA_KV_RANK), wu, cos, sin)


def _mla_q_kernel(n_heads, x_ref, mod_ref, g_ref, wd_ref, gc_ref, wu_ref, cos_ref, sin_ref, q_ref):
    half = MLA_ROPE // 2
    mod = mod_ref[0]
    h = _norm_mod(x_ref[...], g_ref[...], mod[0:1], mod[1:2]).astype(BF16)
    cq = _rms(_dot(h, wd_ref[...]), gc_ref[...]).astype(BF16)
    q = _dot(cq, wu_ref[...])
    n1 = n_heads * MLA_NOPE
    n2 = n1 + n_heads * half
    cos = jnp.concatenate([cos_ref[...]] * n_heads, axis=-1)
    sin = jnp.concatenate([sin_ref[...]] * n_heads, axis=-1)
    r1, r2 = _rope_halves(q[:, n1:n2], q[:, n2:], cos, sin)
    scale = math.log2(math.e) / math.sqrt(MLA_NOPE + MLA_ROPE)
    for hh in range(n_heads):
        q_ref[0, hh] = (jnp.concatenate(
            [q[:, hh * MLA_NOPE:(hh + 1) * MLA_NOPE],
             r1[:, hh * half:(hh + 1) * half], r2[:, hh * half:(hh + 1) * half]], axis=-1) * scale).astype(BF16)


def _mla_q(x2, mod, g_mix, w_dq, g_cq, w_uq, cos, sin, bsz, seq):
    n, d = x2.shape
    q_rank = w_dq.shape[1]
    qk = MLA_NOPE + MLA_ROPE
    n_heads = w_uq.shape[1] // qk
    half = MLA_ROPE // 2
    tm = 512
    tiles = seq // tm
    w3 = w_uq.reshape(q_rank, n_heads, qk)
    wu = jnp.concatenate([w3[:, :, :MLA_NOPE].reshape(q_rank, -1),
                          w3[:, :, MLA_NOPE:MLA_NOPE + half].reshape(q_rank, -1),
                          w3[:, :, MLA_NOPE + half:].reshape(q_rank, -1)], axis=1).astype(BF16)
    row = lambda b, j: (b * tiles + j, 0)
    const = lambda b, j: (0, 0)
    return pl.pallas_call(
        functools.partial(_mla_q_kernel, n_heads),
        grid=(bsz, tiles),
        in_specs=[
            pl.BlockSpec((tm, d), row),
            pl.BlockSpec((1, 6, d), lambda b, j: (b, 0, 0)),
            pl.BlockSpec((1, d), const),
            pl.BlockSpec(w_dq.shape, const),
            pl.BlockSpec((1, q_rank), const),
            pl.BlockSpec(wu.shape, const),
            pl.BlockSpec((tm, half), row),
            pl.BlockSpec((tm, half), row),
        ],
        out_specs=pl.BlockSpec((1, n_heads, tm, qk), lambda b, j: (b, 0, j, 0)),
        out_shape=jax.ShapeDtypeStruct((bsz, n_heads, seq, qk), BF16),
        compiler_params=_params(("arbitrary", "arbitrary")),
        name="mla_q",
    )(x2, mod, g_mix.reshape(1, d), w_dq.astype(BF16), g_cq.reshape(1, q_rank), wu, cos, sin)


ATTN_BLOCK = 512


def _flash_kernel(q_ref, k_ref, v_ref, o_ref):
    t = ATTN_BLOCK
    qi = pl.program_id(2)
    q = q_ref[0, 0]
    dv = v_ref.shape[3]

    def block(carry, start, width, diagonal):
        m_prev, l_prev, acc = carry
        k = k_ref[0, 0, pl.ds(start, width), :]
        v = v_ref[0, 0, pl.ds(start, width), :]
        s = _dot_nt(q, k)
        if diagonal:
            row = lax.broadcasted_iota(I32, (t, width), 0)
            col = lax.broadcasted_iota(I32, (t, width), 1)
            s = jnp.where(col <= row, s, NEG_BIG)
        m_new = jnp.maximum(m_prev, jnp.max(s, axis=-1, keepdims=True))
        alpha = jnp.exp2(m_prev - m_new)
        pr = jnp.exp2(s - m_new)
        l_new = alpha * l_prev + jnp.sum(pr, axis=-1, keepdims=True)
        acc = alpha * acc + _dot(pr.astype(BF16), v)
        return m_new, l_new, acc

    init = (jnp.full((t, 1), NEG_BIG, F32), jnp.zeros((t, 1), F32), jnp.zeros((t, dv), F32))
    carry = lax.fori_loop(
        0, lax.shift_right_logical(qi, 1),
        lambda jj, c: block(c, pl.multiple_of(jj * (2 * t), 2 * t), 2 * t, False), init)
    carry = lax.cond(
        (qi & 1) == 1,
        lambda c: block(c, pl.multiple_of((qi - 1) * t, t), t, False),
        lambda c: c, carry)
    _, l_fin, acc = block(carry, pl.multiple_of(qi * t, t), t, True)
    o_ref[0] = (acc / l_fin).astype(BF16)


def _flash_attn(q, k, v):
    bsz, n_heads, seq, qk = q.shape
    dv = v.shape[3]
    t = ATTN_BLOCK
    return pl.pallas_call(
        _flash_kernel,
        grid=(bsz, n_heads, seq // t),
        in_specs=[
            pl.BlockSpec((1, 1, t, qk), lambda b, h, i: (b, h, i, 0)),
            pl.BlockSpec((1, 1, seq, qk), lambda b, h, i: (b, h, 0, 0)),
            pl.BlockSpec((1, 1, seq, dv), lambda b, h, i: (b, h, 0, 0)),
        ],
        out_specs=pl.BlockSpec((1, t, dv), lambda b, h, i: (b, i, h)),
        out_shape=jax.ShapeDtypeStruct((bsz, seq, n_heads * dv), BF16),
        compiler_params=_params(("arbitrary", "arbitrary", "arbitrary")),
        name="flash_attn",
    )(q, k, v)


def _attn_out_kernel(x_ref, mod_ref, o_ref, w_ref, y_ref):
    y_ref[...] = x_ref[...] + mod_ref[0][2:3] * _dot(o_ref[...], w_ref[...])


def _attn_out(x2, mod, o2, w_o_bf, seq):
    n, d = x2.shape
    tm = 512
    tiles_per_b = seq // tm
    return pl.pallas_call(
        _attn_out_kernel,
        grid=(n // tm,),
        in_specs=[
            pl.BlockSpec((tm, d), lambda i: (i, 0)),
            pl.BlockSpec((1, 6, d), lambda i: (i // tiles_per_b, 0, 0)),
            pl.BlockSpec((tm, o2.shape[1]), lambda i: (i, 0)),
            pl.BlockSpec(w_o_bf.shape, lambda i: (0, 0)),
        ],
        out_specs=pl.BlockSpec((tm, d), lambda i: (i, 0)),
        out_shape=jax.ShapeDtypeStruct((n, d), F32),
        compiler_params=_params(("arbitrary",)),
        name="attn_out",
    )(x2, mod, o2, w_o_bf)


def _final_kernel(x_ref, g_ref, o_ref):
    o_ref[...] = _rms(x_ref[...], g_ref[...])


def _final_norm(x2, g):
    n, d = x2.shape
    tm = 1024
    return pl.pallas_call(
        _final_kernel,
        grid=(n // tm,),
        in_specs=[pl.BlockSpec((tm, d), lambda i: (i, 0)), pl.BlockSpec((1, d), lambda i: (0, 0))],
        out_specs=pl.BlockSpec((tm, d), lambda i: (i, 0)),
        out_shape=jax.ShapeDtypeStruct((n, d), F32),
        compiler_params=_params(("arbitrary",)),
        name="final_norm",
    )(x2, g.reshape(1, d))


def kernel(x, c, positions, g_mix, g_ffn, w_ada, b_ada, w_in_a, lb_logits, g_out_a, w_out_a, g_kv, w_ada_kv, b_ada_kv, w_dkv, g_ckv, w_ukv, w_dq, g_cq, w_uq, w_o_b, w_router, b_router, w_gu, b_gu, w_dn, b_dn, g_final):
    bsz, seq, d = x.shape
    depth = g_mix.shape[0]
    n_a = w_in_a.shape[0]
    n = bsz * seq
    x2 = x.reshape(n, d)

    mods = _ada_mod(c, w_ada, b_ada)
    mods = mods.reshape(depth, bsz, 6, d)
    mod_kv = _ada_mod(c, w_ada_kv[None], b_ada_kv[None]).reshape(bsz, 2, d)

    shared = None
    cos = sin = None
    for l in range(depth):
        mod = mods[l]
        if l < n_a:
            q, k, v, lf, gate = _hgrn_in(x2, mod, g_mix[l], lb_logits, w_in_a[l].astype(BF16), l, seq)
            x2 = _hgrn_rec(x2, mod, q, k, v, lf, gate, g_out_a[l], w_out_a[l].astype(BF16), bsz, seq)
        else:
            j = l - n_a
            qh = _mla_q(x2, mod, g_mix[l], w_dq[j], g_cq[j], w_uq[j], cos, sin, bsz, seq)
            o2 = _flash_attn(qh, *shared)
            x2 = _attn_out(x2, mod, o2.reshape(n, -1), w_o_b[j].astype(BF16), seq)
        x2 = _moe_layer(x2, mod, g_ffn[l], w_router[l], b_router[l],
                        w_gu[l], b_gu[l], w_dn[l], b_dn[l], seq)
        if l == n_a - 1:
            cos, sin = _rope_table(positions)
            shared = _kv_shared(x2, mod_kv, g_kv, w_dkv, g_ckv, w_ukv, cos, sin, bsz, seq)
    return _final_norm(x2, g_final).reshape(bsz, seq, d)
```

```python
import functools
import math

import numpy as np
import jax
import jax.numpy as jnp
from jax import lax
from jax.experimental import pallas as pl
from jax.experimental.pallas import tpu as pltpu

F32 = jnp.float32
BF16 = jnp.bfloat16
I32 = jnp.int32

LANES = 128
SUBLANES = 8
VMEM_LIMIT = 48 * 1024 * 1024
EXPERT_VMEM_LIMIT = 56 * 1024 * 1024

HG_HEAD_DIM = 128
HG_CHUNK = 64
FORGET_FLOOR = 1e-30
MLA_NOPE = 128
MLA_ROPE = 64
MLA_V = 128
MLA_KV_RANK = 256
ROPE_THETA = 10000.0
N_EXPERTS = 32
TOP_K = 4
SWIGLU_LIMIT = 7.0
SWIGLU_ALPHA = 1.702
MOE_BLOCK = 256
NORM_EPS = 1e-6
NEG_BIG = -1e30

N_LEVELS = int(math.log2(HG_CHUNK))


def _params(sem, vmem=VMEM_LIMIT):
    return pltpu.CompilerParams(dimension_semantics=sem, vmem_limit_bytes=vmem)


def _dot(a, b):
    return jnp.dot(a, b, preferred_element_type=F32)


def _dot_nt(a, b):
    return lax.dot_general(a, b, (((1,), (1,)), ((), ())), preferred_element_type=F32)


def _split2(a):
    hi = a.astype(BF16)
    lo = (a - hi.astype(F32)).astype(BF16)
    return hi, lo


def _dot3(a, b):
    a_hi, a_lo = _split2(a)
    b_hi, b_lo = _split2(b)
    return _dot(a_hi, b_hi) + _dot(a_lo, b_hi) + _dot(a_hi, b_lo)


def _sigmoid(x):
    return 1.0 / (1.0 + jnp.exp(-x))


def _rms(x, gain):
    ms = jnp.mean(x * x, axis=-1, keepdims=True)
    return x * lax.rsqrt(ms + NORM_EPS) * gain


def _norm_mod(x, gain, shift, scale):
    return _rms(x, gain) * (1.0 + scale) + shift


def _ada_kernel(c_ref, w_ref, b_ref, o_ref):
    c = c_ref[...]
    c_act = c * _sigmoid(c)
    o_ref[0] = _dot3(c_act, w_ref[0]) + b_ref[0]


def _ada_mod(c, w, b):
    n_l, d, m = w.shape
    bsz = c.shape[0]
    tn = 1024
    return pl.pallas_call(
        _ada_kernel,
        grid=(n_l, m // tn),
        in_specs=[
            pl.BlockSpec((bsz, d), lambda l, j: (0, 0)),
            pl.BlockSpec((1, d, tn), lambda l, j: (l, 0, j)),
            pl.BlockSpec((1, 1, tn), lambda l, j: (l, 0, j)),
        ],
        out_specs=pl.BlockSpec((1, bsz, tn), lambda l, j: (l, 0, j)),
        out_shape=jax.ShapeDtypeStruct((n_l, bsz, m), F32),
        compiler_params=_params(("arbitrary", "arbitrary")),
        name="ada_mod",
    )(c, w, b.reshape(n_l, 1, m))


def _hgrn_in_kernel(layer, n_a, x_ref, mod_ref, g_ref, lbl_ref, w_ref,
                    q_ref, k_ref, v_ref, lf_ref, gate_ref):
    d = x_ref.shape[1]
    mod = mod_ref[0]
    h = _norm_mod(x_ref[...], g_ref[...], mod[0:1], mod[1:2]).astype(BF16)
    rows = [lbl_ref[i:i + 1, :] for i in range(n_a)]
    mx = functools.reduce(jnp.maximum, rows)
    ex = [jnp.exp(r - mx) for r in rows]
    den = functools.reduce(lambda a, b: a + b, ex)
    lb = (functools.reduce(lambda a, b: a + b, ex[:layer + 1]) - ex[0]) / den

    yq = _dot(h, w_ref[:, 0:d])
    q_ref[...] = (yq * _sigmoid(yq)).astype(BF16)
    yf = _dot(h, w_ref[:, d:2 * d])
    sg = _sigmoid(yf)
    forget = lb + (1.0 - lb) * sg
    lf_ref[...] = jnp.log(jnp.maximum(forget, FORGET_FLOOR))
    k_ref[...] = ((1.0 - lb) * (1.0 - sg)).astype(BF16)
    v_ref[...] = _dot(h, w_ref[:, 2 * d:3 * d]).astype(BF16)
    yg = _dot(h, w_ref[:, 3 * d:4 * d])
    gate_ref[...] = (yg * _sigmoid(yg)).astype(BF16)


def _hgrn_in(x2, mod, g_mix, lb_logits, w_in_bf, layer, seq):
    n, d = x2.shape
    tm = 512
    n_a = lb_logits.shape[0]
    tiles_per_b = seq // tm
    row = lambda i: (i, 0)
    return pl.pallas_call(
        functools.partial(_hgrn_in_kernel, layer, n_a),
        grid=(n // tm,),
        in_specs=[
            pl.BlockSpec((tm, d), row),
            pl.BlockSpec((1, 6, d), lambda i: (i // tiles_per_b, 0, 0)),
            pl.BlockSpec((1, d), lambda i: (0, 0)),
            pl.BlockSpec((n_a, d), lambda i: (0, 0)),
            pl.BlockSpec((d, 4 * d), lambda i: (0, 0)),
        ],
        out_specs=[pl.BlockSpec((tm, d), row)] * 5,
        out_shape=[jax.ShapeDtypeStruct((n, d), BF16)] * 3
        + [jax.ShapeDtypeStruct((n, d), F32), jax.ShapeDtypeStruct((n, d), BF16)],
        compiler_params=_params(("arbitrary",)),
        name="hgrn_in",
    )(x2, mod, g_mix.reshape(1, d), lb_logits, w_in_bf)


def _mid_rows(cum, half):
    c, d = cum.shape
    blk = 2 * half
    if blk >= SUBLANES:
        parts = [jnp.broadcast_to(cum[b * blk + half - 1:b * blk + half, :], (blk, d))
                 for b in range(c // blk)]
        return jnp.concatenate(parts, axis=0)
    sub = lax.broadcasted_iota(I32, (SUBLANES, 1), 0)
    parts = []
    for g in range(c // SUBLANES):
        out = None
        for j in range(SUBLANES // blk):
            r = g * SUBLANES + j * blk + half - 1
            cand = jnp.broadcast_to(cum[r:r + 1, :], (SUBLANES, d))
            out = cand if out is None else jnp.where(sub >= j * blk, cand, out)
        parts.append(out)
    return jnp.concatenate(parts, axis=0)


def _hgrn_rec_kernel(n_heads, x_ref, mod_ref, q_ref, k_ref, v_ref, lf_ref, gate_ref,
                     tri_ref, gout_ref, wout_ref, o_ref, state_ref, obuf_ref):
    c = HG_CHUNK
    hd = HG_HEAD_DIM
    ts = x_ref.shape[0]

    @pl.when(pl.program_id(1) == 0)
    def _():
        state_ref[...] = jnp.zeros_like(state_ref)

    t_idx = lax.broadcasted_iota(I32, (c, c), 0)
    s_idx = lax.broadcasted_iota(I32, (c, c), 1)
    r_idx = lax.broadcasted_iota(I32, (c, 1), 0)
    pair_masks = []
    right_rows = []
    for lvl in range(N_LEVELS):
        pair_masks.append(jnp.logical_and(t_idx > s_idx, ((t_idx ^ s_idx) >> lvl) == 1))
        right_rows.append(((r_idx >> lvl) & 1) == 1)
    diag_mask = t_idx == s_idx
    tri = tri_ref[...]
    gout = gout_ref[...]

    def chunk_body(ci, carry):
        r0 = pl.multiple_of(ci * c, c)
        rows = pl.ds(r0, c)
        lf = lf_ref[rows, :]
        qa = q_ref[rows, :].astype(F32)
        ka = k_ref[rows, :].astype(F32)
        va = v_ref[rows, :].astype(F32)
        l_hi = lf.astype(BF16)
        rem = lf - l_hi.astype(F32)
        l_mid = rem.astype(BF16)
        l_lo = (rem - l_mid.astype(F32)).astype(BF16)
        cum = _dot(tri, l_hi) + _dot(tri, l_mid) + _dot(tri, l_lo)
        e_cum = jnp.exp(cum)
        e_tail = jnp.exp(jnp.broadcast_to(cum[c - 1:c, :], cum.shape) - cum)
        q_in = (qa * e_cum).astype(BF16)
        k_out = (ka * e_tail).astype(BF16)
        vb = va.astype(BF16)
        qk_diag = qa * ka
        qk_sub = qa * jnp.exp(lf) * pltpu.roll(ka, 1, 0)
        xs = []
        for lvl in range(1, N_LEVELS):
            half = 1 << lvl
            if half >= SUBLANES:
                r_parts, x_parts = [], []
                for lo in range(0, c, 2 * half):
                    mid, hi = lo + half, lo + 2 * half
                    m_row = jnp.broadcast_to(cum[mid - 1:mid, :], (half, cum.shape[1]))
                    r_parts += [m_row - cum[lo:mid, :], cum[mid:hi, :] - m_row]
                    x_parts += [ka[lo:mid, :], qa[mid:hi, :]]
                e = jnp.exp(jnp.concatenate(r_parts, axis=0))
                xs.append((jnp.concatenate(x_parts, axis=0) * e).astype(BF16))
            else:
                e = jnp.exp(-jnp.abs(cum - _mid_rows(cum, half)))
                xs.append((jnp.where(right_rows[lvl], qa, ka) * e).astype(BF16))
        for h in range(n_heads):
            cols = slice(h * hd, (h + 1) * hd)
            d_col = jnp.sum(qk_diag[:, cols], axis=-1, keepdims=True)
            s_col = jnp.sum(qk_sub[:, cols], axis=-1, keepdims=True)
            sc = jnp.where(diag_mask, d_col, jnp.where(pair_masks[0], s_col, 0.0))
            for lvl in range(1, N_LEVELS):
                xl = xs[lvl - 1][:, cols]
                sc = jnp.where(pair_masks[lvl], _dot_nt(xl, xl), sc)
            st = state_ref[h]
            o = _dot_nt(q_in[:, cols], st.astype(BF16)) + _dot(sc.astype(BF16), vb[:, cols])
            upd = _dot(va[:, cols].T.astype(BF16), k_out[:, cols])
            state_ref[h] = st * e_cum[c - 1:c, cols] + upd
            on = _rms(o, gout) * gate_ref[rows, cols].astype(F32)
            obuf_ref[rows, cols] = on.astype(BF16)
        return carry

    lax.fori_loop(0, ts // c, chunk_body, 0)
    y = _dot(obuf_ref[...], wout_ref[...])
    o_ref[...] = x_ref[...] + mod_ref[0][2:3] * y


def _hgrn_rec(x2, mod, q, k, v, lf, gate, g_out, w_out_bf, bsz, seq):
    n, d = x2.shape
    ts = 512
    n_heads = d // HG_HEAD_DIM
    tiles = seq // ts
    row = lambda b, j: (b * tiles + j, 0)
    const = lambda b, j: (0, 0)
    tri = jnp.asarray(np.tril(np.ones((HG_CHUNK, HG_CHUNK), np.float32)), BF16)
    return pl.pallas_call(
        functools.partial(_hgrn_rec_kernel, n_heads),
        grid=(bsz, tiles),
        in_specs=[
            pl.BlockSpec((ts, d), row),
            pl.BlockSpec((1, 6, d), lambda b, j: (b, 0, 0)),
            pl.BlockSpec((ts, d), row), pl.BlockSpec((ts, d), row), pl.BlockSpec((ts, d), row),
            pl.BlockSpec((ts, d), row), pl.BlockSpec((ts, d), row),
            pl.BlockSpec(tri.shape, const),
            pl.BlockSpec((1, HG_HEAD_DIM), const),
            pl.BlockSpec((d, d), const),
        ],
        out_specs=pl.BlockSpec((ts, d), row),
        out_shape=jax.ShapeDtypeStruct((n, d), F32),
        scratch_shapes=[
            pltpu.VMEM((n_heads, HG_HEAD_DIM, HG_HEAD_DIM), F32),
            pltpu.VMEM((ts, d), BF16),
        ],
        compiler_params=_params(("arbitrary", "arbitrary")),
        name="hgrn_rec",
    )(x2, mod, q, k, v, lf, gate, tri, g_out.reshape(1, HG_HEAD_DIM), w_out_bf)


def _store_token_tiles(ref, val):
    tm = val.shape[0]
    for cidx in range(val.shape[1] // LANES):
        ref[pl.ds(cidx, tm, stride=SUBLANES), :] = val[:, cidx * LANES:(cidx + 1) * LANES]


def _load_token_tiles(ref, tm):
    n_c = ref.shape[0] // tm
    return jnp.concatenate([ref[pl.ds(cidx, tm, stride=SUBLANES), :] for cidx in range(n_c)], axis=-1)


def _router_kernel(x_ref, mod_ref, g_ref, wr_ref, br_ref, h_ref, idx_ref, wt_ref, cnt_ref):
    tm = x_ref.shape[0]
    mod = mod_ref[0]
    h = _norm_mod(x_ref[...], g_ref[...], mod[3:4], mod[4:5])
    _store_token_tiles(h_ref, h)
    logits = _dot3(h, wr_ref[...]) + br_ref[...]
    lane = lax.broadcasted_iota(I32, (tm, LANES), 1)
    lane_f = lane.astype(F32)
    idx_acc = jnp.zeros((tm, LANES), F32)
    val_acc = jnp.zeros((tm, LANES), F32)
    cnt = jnp.zeros((tm, LANES), F32)
    work = logits
    for kk in range(TOP_K):
        m = jnp.max(work, axis=-1, keepdims=True)
        sel = jnp.min(jnp.where(work == m, lane_f, float(LANES)), axis=-1, keepdims=True)
        hit = lane_f == sel
        work = jnp.where(hit, -jnp.inf, work)
        cnt = cnt + jnp.where(hit, 1.0, 0.0)
        idx_acc = jnp.where(lane == kk, sel, idx_acc)
        val_acc = jnp.where(lane == kk, m, val_acc)
    top = jnp.max(jnp.where(lane < TOP_K, val_acc, -jnp.inf), axis=-1, keepdims=True)
    ex = jnp.where(lane < TOP_K, jnp.exp(val_acc - top), 0.0)
    wts = ex / jnp.sum(ex, axis=-1, keepdims=True)
    idx_ref[...] = idx_acc[:, :TOP_K].astype(I32)
    wt_ref[...] = wts[:, :TOP_K]

    @pl.when(pl.program_id(0) == 0)
    def _():
        cnt_ref[...] = jnp.zeros_like(cnt_ref)

    cnt_ref[...] += jnp.sum(cnt, axis=0, keepdims=True)


def _moe_router(x2, mod, g_ffn, w_router, b_router, seq):
    n, d = x2.shape
    tm = 512
    tiles_per_b = seq // tm
    wr = jnp.zeros((d, LANES), F32).at[:, :N_EXPERTS].set(w_router)
    br = jnp.full((1, LANES), NEG_BIG, F32).at[0, :N_EXPERTS].set(b_router)
    return pl.pallas_call(
        _router_kernel,
        grid=(n // tm,),
        in_specs=[
            pl.BlockSpec((tm, d), lambda i: (i, 0)),
            pl.BlockSpec((1, 6, d), lambda i: (i // tiles_per_b, 0, 0)),
            pl.BlockSpec((1, d), lambda i: (0, 0)),
            pl.BlockSpec((d, LANES), lambda i: (0, 0)),
            pl.BlockSpec((1, LANES), lambda i: (0, 0)),
        ],
        out_specs=[
            pl.BlockSpec((tm * SUBLANES, LANES), lambda i: (i, 0)),
            pl.BlockSpec((tm, TOP_K), lambda i: (i, 0)),
            pl.BlockSpec((tm, TOP_K), lambda i: (i, 0)),
            pl.BlockSpec((1, LANES), lambda i: (0, 0)),
        ],
        out_shape=[
            jax.ShapeDtypeStruct((n * SUBLANES, LANES), F32),
            jax.ShapeDtypeStruct((n, TOP_K), I32),
            jax.ShapeDtypeStruct((n, TOP_K), F32),
            jax.ShapeDtypeStruct((1, LANES), F32),
        ],
        compiler_params=_params(("arbitrary",)),
        name="moe_router",
    )(x2, mod, g_ffn.reshape(1, d), wr, br)


def _dest_kernel(idx_ref, start_ref, tri_ref, dest_ref, carry_ref):
    tm = idx_ref.shape[0]

    @pl.when(pl.program_id(0) == 0)
    def _():
        carry_ref[...] = jnp.zeros_like(carry_ref)

    lane = lax.broadcasted_iota(I32, (tm, LANES), 1)
    idx = idx_ref[...]
    hits = [lane == idx[:, kk:kk + 1] for kk in range(TOP_K)]
    cnt = functools.reduce(lambda a, b: a + b, [jnp.where(hh, 1.0, 0.0) for hh in hits])
    before = _dot(tri_ref[...], cnt.astype(BF16))
    pos = before + carry_ref[...] + start_ref[...]
    acc = jnp.zeros((tm, LANES), F32)
    for kk in range(TOP_K):
        dk = jnp.sum(jnp.where(hits[kk], pos, 0.0), axis=-1, keepdims=True)
        acc = jnp.where(lane == kk, dk, acc)
    dest_ref[...] = acc[:, :TOP_K].astype(I32)
    carry_ref[...] += jnp.sum(cnt, axis=0, keepdims=True)


def _moe_dest(idx, pad_start):
    n = idx.shape[0]
    tm = 512
    tri = jnp.asarray(np.tril(np.ones((tm, tm), np.float32), -1), BF16)
    return pl.pallas_call(
        _dest_kernel,
        grid=(n // tm,),
        in_specs=[
            pl.BlockSpec((tm, TOP_K), lambda i: (i, 0)),
            pl.BlockSpec((1, LANES), lambda i: (0, 0)),
            pl.BlockSpec((tm, tm), lambda i: (0, 0)),
        ],
        out_specs=pl.BlockSpec((tm, TOP_K), lambda i: (i, 0)),
        out_shape=jax.ShapeDtypeStruct((n, TOP_K), I32),
        scratch_shapes=[pltpu.VMEM((1, LANES), F32)],
        compiler_params=_params(("arbitrary",)),
        name="moe_dest",
    )(idx, pad_start, tri)


DISPATCH_TOKENS = 256


def _token_copy(src, src_tok, dst, dst_tok, sem):
    s0 = pl.multiple_of(src_tok * SUBLANES, SUBLANES)
    d0 = pl.multiple_of(dst_tok * SUBLANES, SUBLANES)
    return pltpu.make_async_copy(src.at[pl.ds(s0, SUBLANES), :], dst.at[pl.ds(d0, SUBLANES), :], sem)


def _dispatch_kernel(gend_ref, gpad_ref, dest_ref, h_ref, xs_ref, zero_ref, sem, zsem):
    t = DISPATCH_TOKENS
    blk_rows = MOE_BLOCK * SUBLANES

    @pl.when(pl.program_id(0) == 0)
    def _():
        zero_ref[...] = jnp.zeros_like(zero_ref)

        def zero_copy(e):
            start = pl.multiple_of((gend_ref[e] - MOE_BLOCK) * SUBLANES, SUBLANES)
            return pltpu.make_async_copy(zero_ref, xs_ref.at[pl.ds(start, blk_rows), :], zsem)

        n_blocks = xs_ref.shape[0] // blk_rows
        first_free = gend_ref[N_EXPERTS - 1] // MOE_BLOCK

        def free_copy(j):
            start = pl.multiple_of((first_free + j) * blk_rows, blk_rows)
            return pltpu.make_async_copy(zero_ref, xs_ref.at[pl.ds(start, blk_rows), :], zsem)

        for e in range(N_EXPERTS):
            @pl.when(gpad_ref[e] > 0)
            def _():
                zero_copy(e).start()

            @pl.when(first_free + e < n_blocks)
            def _():
                free_copy(e).start()
        for e in range(N_EXPERTS):
            @pl.when(gpad_ref[e] > 0)
            def _():
                zero_copy(e).wait()

            @pl.when(first_free + e < n_blocks)
            def _():
                free_copy(e).wait()

    def start(tok, carry):
        for kk in range(TOP_K):
            _token_copy(h_ref, tok, xs_ref, dest_ref[tok * TOP_K + kk], sem).start(priority=kk % 2)
        return carry

    lax.fori_loop(0, t, start, 0, unroll=4)
    for _ in range(TOP_K):
        pltpu.make_async_copy(h_ref, xs_ref.at[pl.ds(0, t * SUBLANES), :], sem).wait()


def _moe_dispatch(h_tt, dest, pad_end, padded, n_rows):
    n = dest.shape[0]
    t = DISPATCH_TOKENS
    grid_spec = pltpu.PrefetchScalarGridSpec(
        num_scalar_prefetch=2,
        grid=(n // t,),
        in_specs=[
            pl.BlockSpec((t * TOP_K,), lambda i, ge, gp: (i,), memory_space=pltpu.SMEM),
            pl.BlockSpec((t * SUBLANES, LANES), lambda i, ge, gp: (i, 0)),
        ],
        out_specs=pl.BlockSpec(memory_space=pl.ANY),
        scratch_shapes=[
            pltpu.VMEM((MOE_BLOCK * SUBLANES, LANES), F32),
            pltpu.SemaphoreType.DMA(()),
            pltpu.SemaphoreType.DMA(()),
        ],
    )
    return pl.pallas_call(
        _dispatch_kernel,
        grid_spec=grid_spec,
        out_shape=jax.ShapeDtypeStruct((n_rows * SUBLANES, LANES), F32),
        compiler_params=_params(("arbitrary",)),
        name="moe_dispatch",
    )(pad_end, padded, dest.reshape(n * TOP_K), h_tt)


def _experts_kernel(be_ref, na_ref, xs_ref, wgu_ref, bgu_ref, wdn_ref, bdn_ref, ys_ref, wgu_bf, wdn_bf):
    i = pl.program_id(0)
    tm = MOE_BLOCK
    ff = wdn_ref.shape[2]

    @pl.when((i == 0) | (be_ref[i] != be_ref[jnp.maximum(i - 1, 0)]))
    def _():
        wgu_bf[...] = wgu_ref[0, 0].astype(BF16)
        wdn_bf[...] = wdn_ref[0, 0].astype(BF16)

    @pl.when(i < na_ref[0])
    def _():
        xb = _load_token_tiles(xs_ref, tm).astype(BF16)
        gu = _dot(xb, wgu_bf[...]) + bgu_ref[0, 0]
        gate = jnp.minimum(gu[:, :ff], SWIGLU_LIMIT)
        up = jnp.clip(gu[:, ff:], -SWIGLU_LIMIT, SWIGLU_LIMIT)
        glu = gate * _sigmoid(SWIGLU_ALPHA * gate)
        y = _dot(((up + 1.0) * glu).astype(BF16), wdn_bf[...]) + bdn_ref[0, 0]
        _store_token_tiles(ys_ref, y)

    @pl.when(i >= na_ref[0])
    def _():
        ys_ref[...] = jnp.zeros_like(ys_ref)


def _moe_experts(xs, block_e, n_active, layer, w_gu, b_gu, w_dn, b_dn):
    n_l, n_e, d, ff2 = w_gu.shape
    ff = ff2 // 2
    n_blocks = block_e.shape[0]
    tm = MOE_BLOCK
    grid_spec = pltpu.PrefetchScalarGridSpec(
        num_scalar_prefetch=2,
        grid=(n_blocks,),
        in_specs=[
            pl.BlockSpec((tm * SUBLANES, LANES), lambda i, be, na: (jnp.minimum(i, na[0] - 1), 0)),
            pl.BlockSpec((1, 1, d, ff2), lambda i, be, na: (layer, be[i], 0, 0)),
            pl.BlockSpec((1, 1, 1, ff2), lambda i, be, na: (layer, be[i], 0, 0)),
            pl.BlockSpec((1, 1, ff, d), lambda i, be, na: (layer, be[i], 0, 0)),
            pl.BlockSpec((1, 1, 1, d), lambda i, be, na: (layer, be[i], 0, 0)),
        ],
        out_specs=pl.BlockSpec((tm * SUBLANES, LANES), lambda i, be, na: (i, 0)),
        scratch_shapes=[pltpu.VMEM((d, ff2), BF16), pltpu.VMEM((ff, d), BF16)],
    )
    return pl.pallas_call(
        _experts_kernel,
        grid_spec=grid_spec,
        out_shape=jax.ShapeDtypeStruct(xs.shape, F32),
        compiler_params=_params(("arbitrary",), EXPERT_VMEM_LIMIT),
        name="moe_experts",
    )(block_e, n_active, xs, w_gu, b_gu.reshape(n_l, n_e, 1, ff2), w_dn, b_dn.reshape(n_l, n_e, 1, d))


def _combine_kernel(dest_ref, x_ref, mod_ref, wt_ref, ys_ref, o_ref, buf_ref, sem):
    t = DISPATCH_TOKENS
    slot_rows = t * SUBLANES

    def start(tok, carry):
        for kk in range(TOP_K):
            _token_copy(ys_ref, dest_ref[tok * TOP_K + kk], buf_ref, kk * t + tok, sem).start(priority=kk % 2)
        return carry

    lax.fori_loop(0, t, start, 0, unroll=4)
    pltpu.make_async_copy(ys_ref.at[pl.ds(0, TOP_K * slot_rows), :], buf_ref, sem).wait()

    wt = wt_ref[...]
    y = jnp.zeros(x_ref.shape, F32)
    for kk in range(TOP_K):
        y = y + _load_token_tiles(buf_ref.at[pl.ds(kk * slot_rows, slot_rows), :], t) * wt[:, kk:kk + 1]
    o_ref[...] = x_ref[...] + mod_ref[0][5:6] * y


def _moe_combine(x2, mod, wts, dest, ys, seq):
    n, d = x2.shape
    t = DISPATCH_TOKENS
    tiles_per_b = seq // t
    return pl.pallas_call(
        _combine_kernel,
        grid=(n // t,),
        in_specs=[
            pl.BlockSpec((t * TOP_K,), lambda i: (i,), memory_space=pltpu.SMEM),
            pl.BlockSpec((t, d), lambda i: (i, 0)),
            pl.BlockSpec((1, 6, d), lambda i: (i // tiles_per_b, 0, 0)),
            pl.BlockSpec((t, TOP_K), lambda i: (i, 0)),
            pl.BlockSpec(memory_space=pl.ANY),
        ],
        out_specs=pl.BlockSpec((t, d), lambda i: (i, 0)),
        out_shape=jax.ShapeDtypeStruct((n, d), F32),
        scratch_shapes=[
            pltpu.VMEM((TOP_K * t * SUBLANES, LANES), F32),
            pltpu.SemaphoreType.DMA(()),
        ],
        compiler_params=_params(("arbitrary",)),
        name="moe_combine",
    )(dest.reshape(n * TOP_K), x2, mod, wts, ys)


def _moe_layer(x2, mod, g_ffn, w_router, b_router, layer, w_gu, b_gu, w_dn, b_dn, seq):
    n = x2.shape[0]
    h_tt, idx, wts, counts = _moe_router(x2, mod, g_ffn, w_router, b_router, seq)
    counts = counts[0, :N_EXPERTS].astype(I32)
    padded = (counts + MOE_BLOCK - 1) // MOE_BLOCK * MOE_BLOCK
    pad_end = jnp.cumsum(padded)
    pad_start = pad_end - padded
    n_rows = n * TOP_K + N_EXPERTS * MOE_BLOCK
    n_blocks = n_rows // MOE_BLOCK
    blk_start = jnp.arange(n_blocks, dtype=I32) * MOE_BLOCK
    block_e = jnp.minimum(jnp.sum((pad_end[None, :] <= blk_start[:, None]).astype(I32), axis=1),
                          N_EXPERTS - 1)
    n_active = (pad_end[-1:] // MOE_BLOCK).astype(I32)
    start_vec = jnp.zeros((1, LANES), F32).at[0, :N_EXPERTS].set(pad_start.astype(F32))
    dest = _moe_dest(idx, start_vec)
    xs = _moe_dispatch(h_tt, dest, pad_end.astype(I32), padded.astype(I32), n_rows)
    ys = _moe_experts(xs, block_e, n_active, layer, w_gu, b_gu, w_dn, b_dn)
    return _moe_combine(x2, mod, wts, dest, ys, seq)


def _rope_kernel(pos_ref, inv_ref, cos_ref, sin_ref):
    ang = pos_ref[...] * inv_ref[...]
    cos_ref[...] = jnp.cos(ang)
    sin_ref[...] = jnp.sin(ang)


def _rope_table(positions):
    half = MLA_ROPE // 2
    n = positions.size
    rep = LANES // half
    inv = np.power(np.float32(ROPE_THETA),
                   -np.arange(half, dtype=np.float32) * np.float32(2.0 / MLA_ROPE)).astype(np.float32)
    inv_row = jnp.asarray(np.tile(inv, rep).reshape(1, LANES))
    pos = jnp.repeat(positions.reshape(n).astype(F32), half).reshape(n // rep, LANES)
    rows = n // rep
    tr = min(rows, 2048)
    cos, sin = pl.pallas_call(
        _rope_kernel,
        grid=(rows // tr,),
        in_specs=[pl.BlockSpec((tr, LANES), lambda i: (i, 0)), pl.BlockSpec((1, LANES), lambda i: (0, 0))],
        out_specs=[pl.BlockSpec((tr, LANES), lambda i: (i, 0))] * 2,
        out_shape=[jax.ShapeDtypeStruct((rows, LANES), F32)] * 2,
        compiler_params=_params(("arbitrary",)),
        name="rope_table",
    )(pos, inv_row)
    return cos.reshape(n, half), sin.reshape(n, half)


def _rope_halves(t1, t2, cos, sin):
    return t1 * cos - t2 * sin, t2 * cos + t1 * sin


def _kv_kernel(n_heads, x_ref, mod_ref, g_ref, wd_ref, gc_ref, wu_ref, cos_ref, sin_ref, k_ref, v_ref):
    half = MLA_ROPE // 2
    mod = mod_ref[0]
    hk = _norm_mod(x_ref[...], g_ref[...], mod[0:1], mod[1:2]).astype(BF16)
    ckr = _dot(hk, wd_ref[...])
    c_kv = _rms(ckr[:, :MLA_KV_RANK], gc_ref[...]).astype(BF16)
    r1, r2 = _rope_halves(ckr[:, MLA_KV_RANK:MLA_KV_RANK + half], ckr[:, MLA_KV_RANK + half:],
                          cos_ref[...], sin_ref[...])
    kv = _dot(c_kv, wu_ref[...])
    nv = n_heads * MLA_NOPE
    for h in range(n_heads):
        k_ref[0, h] = jnp.concatenate(
            [kv[:, h * MLA_NOPE:(h + 1) * MLA_NOPE], r1, r2], axis=-1).astype(BF16)
        v_ref[0, h] = kv[:, nv + h * MLA_V:nv + (h + 1) * MLA_V].astype(BF16)


def _kv_shared(x2, mod_kv, g_kv, w_dkv, g_ckv, w_ukv, cos, sin, bsz, seq):
    n, d = x2.shape
    n_heads = w_ukv.shape[1] // (MLA_NOPE + MLA_V)
    half = MLA_ROPE // 2
    tm = 512
    tiles = seq // tm
    w3 = w_ukv.reshape(MLA_KV_RANK, n_heads, MLA_NOPE + MLA_V)
    wu = jnp.concatenate([w3[:, :, :MLA_NOPE].reshape(MLA_KV_RANK, -1),
                          w3[:, :, MLA_NOPE:].reshape(MLA_KV_RANK, -1)], axis=1).astype(BF16)
    row = lambda b, j: (b * tiles + j, 0)
    const = lambda b, j: (0, 0)
    qk = MLA_NOPE + MLA_ROPE
    return pl.pallas_call(
        functools.partial(_kv_kernel, n_heads),
        grid=(bsz, tiles),
        in_specs=[
            pl.BlockSpec((tm, d), row),
            pl.BlockSpec((1, 2, d), lambda b, j: (b, 0, 0)),
            pl.BlockSpec((1, d), const),
            pl.BlockSpec(w_dkv.shape, const),
            pl.BlockSpec((1, MLA_KV_RANK), const),
            pl.BlockSpec(wu.shape, const),
            pl.BlockSpec((tm, half), row),
            pl.BlockSpec((tm, half), row),
        ],
        out_specs=[
            pl.BlockSpec((1, n_heads, tm, qk), lambda b, j: (b, 0, j, 0)),
            pl.BlockSpec((1, n_heads, tm, MLA_V), lambda b, j: (b, 0, j, 0)),
        ],
        out_shape=[
            jax.ShapeDtypeStruct((bsz, n_heads, seq, qk), BF16),
            jax.ShapeDtypeStruct((bsz, n_heads, seq, MLA_V), BF16),
        ],
        compiler_params=_params(("arbitrary", "arbitrary")),
        name="kv_shared",
    )(x2, mod_kv, g_kv.reshape(1, d), w_dkv.astype(BF16), g_ckv.reshape(1, MLA_KV_RANK), wu, cos, sin)


def _mla_q_kernel(n_heads, x_ref, mod_ref, g_ref, wd_ref, gc_ref, wu_ref, cos_ref, sin_ref, q_ref):
    half = MLA_ROPE // 2
    mod = mod_ref[0]
    h = _norm_mod(x_ref[...], g_ref[...], mod[0:1], mod[1:2]).astype(BF16)
    cq = _rms(_dot(h, wd_ref[...]), gc_ref[...]).astype(BF16)
    q = _dot(cq, wu_ref[...])
    n1 = n_heads * MLA_NOPE
    n2 = n1 + n_heads * half
    cos = jnp.concatenate([cos_ref[...]] * n_heads, axis=-1)
    sin = jnp.concatenate([sin_ref[...]] * n_heads, axis=-1)
    r1, r2 = _rope_halves(q[:, n1:n2], q[:, n2:], cos, sin)
    scale = math.log2(math.e) / math.sqrt(MLA_NOPE + MLA_ROPE)
    for hh in range(n_heads):
        q_ref[0, hh] = (jnp.concatenate(
            [q[:, hh * MLA_NOPE:(hh + 1) * MLA_NOPE],
             r1[:, hh * half:(hh + 1) * half], r2[:, hh * half:(hh + 1) * half]], axis=-1) * scale).astype(BF16)


def _mla_q(x2, mod, g_mix, w_dq, g_cq, w_uq, cos, sin, bsz, seq):
    n, d = x2.shape
    q_rank = w_dq.shape[1]
    qk = MLA_NOPE + MLA_ROPE
    n_heads = w_uq.shape[1] // qk
    half = MLA_ROPE // 2
    tm = 512
    tiles = seq // tm
    w3 = w_uq.reshape(q_rank, n_heads, qk)
    wu = jnp.concatenate([w3[:, :, :MLA_NOPE].reshape(q_rank, -1),
                          w3[:, :, MLA_NOPE:MLA_NOPE + half].reshape(q_rank, -1),
                          w3[:, :, MLA_NOPE + half:].reshape(q_rank, -1)], axis=1).astype(BF16)
    row = lambda b, j: (b * tiles + j, 0)
    const = lambda b, j: (0, 0)
    return pl.pallas_call(
        functools.partial(_mla_q_kernel, n_heads),
        grid=(bsz, tiles),
        in_specs=[
            pl.BlockSpec((tm, d), row),
            pl.BlockSpec((1, 6, d), lambda b, j: (b, 0, 0)),
            pl.BlockSpec((1, d), const),
            pl.BlockSpec(w_dq.shape, const),
            pl.BlockSpec((1, q_rank), const),
            pl.BlockSpec(wu.shape, const),
            pl.BlockSpec((tm, half), row),
            pl.BlockSpec((tm, half), row),
        ],
        out_specs=pl.BlockSpec((1, n_heads, tm, qk), lambda b, j: (b, 0, j, 0)),
        out_shape=jax.ShapeDtypeStruct((bsz, n_heads, seq, qk), BF16),
        compiler_params=_params(("arbitrary", "arbitrary")),
        name="mla_q",
    )(x2, mod, g_mix.reshape(1, d), w_dq.astype(BF16), g_cq.reshape(1, q_rank), wu, cos, sin)


ATTN_BLOCK = 512
ATTN_SUB_BLOCKS = 1


def _flash_kernel(q_ref, k_ref, v_ref, o_ref):
    t = ATTN_BLOCK
    n_sub = ATTN_SUB_BLOCKS
    ts = t // n_sub
    qi = pl.program_id(2)
    dv = v_ref.shape[3]
    q_subs = [q_ref[0, 0, i * ts:(i + 1) * ts, :] for i in range(n_sub)]

    def block(carry, start, width, diagonal):
        k = k_ref[0, 0, pl.ds(start, width), :]
        v = v_ref[0, 0, pl.ds(start, width), :]
        out = []
        for i, (m_prev, l_prev, acc) in enumerate(carry):
            s = _dot_nt(q_subs[i], k)
            if diagonal:
                row = lax.broadcasted_iota(I32, (ts, width), 0) + i * ts
                col = lax.broadcasted_iota(I32, (ts, width), 1)
                s = jnp.where(col <= row, s, NEG_BIG)
            m_new = jnp.maximum(m_prev, jnp.max(s, axis=-1, keepdims=True))
            alpha = jnp.exp2(m_prev - m_new)
            pr = jnp.exp2(s - m_new)
            l_new = alpha * l_prev + jnp.sum(pr, axis=-1, keepdims=True)
            out.append((m_new, l_new, alpha * acc + _dot(pr.astype(BF16), v)))
        return tuple(out)

    init = tuple((jnp.full((ts, 1), NEG_BIG, F32), jnp.zeros((ts, 1), F32), jnp.zeros((ts, dv), F32))
                 for _ in range(n_sub))
    carry = lax.fori_loop(
        0, lax.shift_right_logical(qi, 1),
        lambda jj, c: block(c, pl.multiple_of(jj * (2 * t), 2 * t), 2 * t, False), init)
    carry = lax.cond(
        (qi & 1) == 1,
        lambda c: block(c, pl.multiple_of((qi - 1) * t, t), t, False),
        lambda c: c, carry)
    carry = block(carry, pl.multiple_of(qi * t, t), t, True)
    for i, (_, l_fin, acc) in enumerate(carry):
        o_ref[0, i * ts:(i + 1) * ts, :] = (acc / l_fin).astype(BF16)


def _flash_attn(q, k, v):
    bsz, n_heads, seq, qk = q.shape
    dv = v.shape[3]
    t = ATTN_BLOCK
    return pl.pallas_call(
        _flash_kernel,
        grid=(bsz, n_heads, seq // t),
        in_specs=[
            pl.BlockSpec((1, 1, t, qk), lambda b, h, i: (b, h, i, 0)),
            pl.BlockSpec((1, 1, seq, qk), lambda b, h, i: (b, h, 0, 0)),
            pl.BlockSpec((1, 1, seq, dv), lambda b, h, i: (b, h, 0, 0)),
        ],
        out_specs=pl.BlockSpec((1, t, dv), lambda b, h, i: (b, i, h)),
        out_shape=jax.ShapeDtypeStruct((bsz, seq, n_heads * dv), BF16),
        compiler_params=_params(("arbitrary", "arbitrary", "arbitrary")),
        name="flash_attn",
    )(q, k, v)


def _attn_out_kernel(x_ref, mod_ref, o_ref, w_ref, y_ref):
    y_ref[...] = x_ref[...] + mod_ref[0][2:3] * _dot(o_ref[...], w_ref[...])


def _attn_out(x2, mod, o2, w_o_bf, seq):
    n, d = x2.shape
    tm = 512
    tiles_per_b = seq // tm
    return pl.pallas_call(
        _attn_out_kernel,
        grid=(n // tm,),
        in_specs=[
            pl.BlockSpec((tm, d), lambda i: (i, 0)),
            pl.BlockSpec((1, 6, d), lambda i: (i // tiles_per_b, 0, 0)),
            pl.BlockSpec((tm, o2.shape[1]), lambda i: (i, 0)),
            pl.BlockSpec(w_o_bf.shape, lambda i: (0, 0)),
        ],
        out_specs=pl.BlockSpec((tm, d), lambda i: (i, 0)),
        out_shape=jax.ShapeDtypeStruct((n, d), F32),
        compiler_params=_params(("arbitrary",)),
        name="attn_out",
    )(x2, mod, o2, w_o_bf)


def _final_kernel(x_ref, g_ref, o_ref):
    o_ref[...] = _rms(x_ref[...], g_ref[...])


def _final_norm(x2, g):
    n, d = x2.shape
    tm = 1024
    return pl.pallas_call(
        _final_kernel,
        grid=(n // tm,),
        in_specs=[pl.BlockSpec((tm, d), lambda i: (i, 0)), pl.BlockSpec((1, d), lambda i: (0, 0))],
        out_specs=pl.BlockSpec((tm, d), lambda i: (i, 0)),
        out_shape=jax.ShapeDtypeStruct((n, d), F32),
        compiler_params=_params(("arbitrary",)),
        name="final_norm",
    )(x2, g.reshape(1, d))


def kernel(x, c, positions, g_mix, g_ffn, w_ada, b_ada, w_in_a, lb_logits, g_out_a, w_out_a, g_kv, w_ada_kv, b_ada_kv, w_dkv, g_ckv, w_ukv, w_dq, g_cq, w_uq, w_o_b, w_router, b_router, w_gu, b_gu, w_dn, b_dn, g_final):
    bsz, seq, d = x.shape
    depth = g_mix.shape[0]
    n_a = w_in_a.shape[0]
    n = bsz * seq
    x2 = x.reshape(n, d)

    mods = _ada_mod(c, w_ada, b_ada)
    mods = mods.reshape(depth, bsz, 6, d)
    mod_kv = _ada_mod(c, w_ada_kv[None], b_ada_kv[None]).reshape(bsz, 2, d)

    shared = None
    cos = sin = None
    for l in range(depth):
        mod = mods[l]
        if l < n_a:
            q, k, v, lf, gate = _hgrn_in(x2, mod, g_mix[l], lb_logits, w_in_a[l].astype(BF16), l, seq)
            x2 = _hgrn_rec(x2, mod, q, k, v, lf, gate, g_out_a[l], w_out_a[l].astype(BF16), bsz, seq)
        else:
            j = l - n_a
            qh = _mla_q(x2, mod, g_mix[l], w_dq[j], g_cq[j], w_uq[j], cos, sin, bsz, seq)
            o2 = _flash_attn(qh, *shared)
            x2 = _attn_out(x2, mod, o2.reshape(n, -1), w_o_b[j].astype(BF16), seq)
        x2 = _moe_layer(x2, mod, g_ffn[l], w_router[l], b_router[l],
                        l, w_gu, b_gu, w_dn, b_dn, seq)
        if l == n_a - 1:
            cos, sin = _rope_table(positions)
            shared = _kv_shared(x2, mod_kv, g_kv, w_dkv, g_ckv, w_ukv, cos, sin, bsz, seq)
    return _final_norm(x2, g_final).reshape(bsz, seq, d)
```

```python
import functools
import math

import numpy as np
import jax
import jax.numpy as jnp
from jax import lax
from jax.experimental import pallas as pl
from jax.experimental.pallas import tpu as pltpu

F32 = jnp.float32
BF16 = jnp.bfloat16
I32 = jnp.int32

LANES = 128
SUBLANES = 8
VMEM_LIMIT = 48 * 1024 * 1024
EXPERT_VMEM_LIMIT = 56 * 1024 * 1024

HG_HEAD_DIM = 128
HG_CHUNK = 256
FORGET_FLOOR = 1e-30
MLA_NOPE = 128
MLA_ROPE = 64
MLA_V = 128
MLA_KV_RANK = 256
ROPE_THETA = 10000.0
N_EXPERTS = 32
TOP_K = 4
SWIGLU_LIMIT = 7.0
SWIGLU_ALPHA = 1.702
MOE_BLOCK = 512
NORM_EPS = 1e-6
NEG_BIG = -1e30

N_LEVELS = int(math.log2(HG_CHUNK))


def _params(sem, vmem=VMEM_LIMIT):
    return pltpu.CompilerParams(dimension_semantics=sem, vmem_limit_bytes=vmem)


def _dot(a, b):
    return jnp.dot(a, b, preferred_element_type=F32)


def _dot_nt(a, b):
    return lax.dot_general(a, b, (((1,), (1,)), ((), ())), preferred_element_type=F32)


def _split2(a):
    hi = a.astype(BF16)
    lo = (a - hi.astype(F32)).astype(BF16)
    return hi, lo


def _dot3(a, b):
    a_hi, a_lo = _split2(a)
    b_hi, b_lo = _split2(b)
    return _dot(a_hi, b_hi) + _dot(a_lo, b_hi) + _dot(a_hi, b_lo)


def _sigmoid(x):
    return 1.0 / (1.0 + jnp.exp(-x))


def _rms(x, gain):
    ms = jnp.mean(x * x, axis=-1, keepdims=True)
    return x * lax.rsqrt(ms + NORM_EPS) * gain


def _norm_mod(x, gain, shift, scale):
    return _rms(x, gain) * (1.0 + scale) + shift


def _ada_kernel(c_ref, w_ref, b_ref, o_ref):
    c = c_ref[...]
    c_act = c * _sigmoid(c)
    o_ref[0] = _dot3(c_act, w_ref[0]) + b_ref[0]


def _ada_mod(c, w, b):
    n_l, d, m = w.shape
    bsz = c.shape[0]
    tn = 1024
    return pl.pallas_call(
        _ada_kernel,
        grid=(n_l, m // tn),
        in_specs=[
            pl.BlockSpec((bsz, d), lambda l, j: (0, 0)),
            pl.BlockSpec((1, d, tn), lambda l, j: (l, 0, j)),
            pl.BlockSpec((1, 1, tn), lambda l, j: (l, 0, j)),
        ],
        out_specs=pl.BlockSpec((1, bsz, tn), lambda l, j: (l, 0, j)),
        out_shape=jax.ShapeDtypeStruct((n_l, bsz, m), F32),
        compiler_params=_params(("arbitrary", "arbitrary")),
        name="ada_mod",
    )(c, w, b.reshape(n_l, 1, m))


def _hgrn_in_kernel(layer, n_a, x_ref, mod_ref, g_ref, lbl_ref, w_ref,
                    q_ref, k_ref, v_ref, lf_ref, gate_ref):
    d = x_ref.shape[1]
    mod = mod_ref[0]
    h = _norm_mod(x_ref[...], g_ref[...], mod[0:1], mod[1:2]).astype(BF16)
    rows = [lbl_ref[i:i + 1, :] for i in range(n_a)]
    mx = functools.reduce(jnp.maximum, rows)
    ex = [jnp.exp(r - mx) for r in rows]
    den = functools.reduce(lambda a, b: a + b, ex)
    lb = (functools.reduce(lambda a, b: a + b, ex[:layer + 1]) - ex[0]) / den

    yq = _dot(h, w_ref[:, 0:d])
    q_ref[...] = (yq * _sigmoid(yq)).astype(BF16)
    yf = _dot(h, w_ref[:, d:2 * d])
    sg = _sigmoid(yf)
    forget = lb + (1.0 - lb) * sg
    lf_ref[...] = jnp.log(jnp.maximum(forget, FORGET_FLOOR))
    k_ref[...] = ((1.0 - lb) * (1.0 - sg)).astype(BF16)
    v_ref[...] = _dot(h, w_ref[:, 2 * d:3 * d]).astype(BF16)
    yg = _dot(h, w_ref[:, 3 * d:4 * d])
    gate_ref[...] = (yg * _sigmoid(yg)).astype(BF16)


def _hgrn_in(x2, mod, g_mix, lb_logits, w_in_bf, layer, seq):
    n, d = x2.shape
    tm = 512
    n_a = lb_logits.shape[0]
    tiles_per_b = seq // tm
    row = lambda i: (i, 0)
    return pl.pallas_call(
        functools.partial(_hgrn_in_kernel, layer, n_a),
        grid=(n // tm,),
        in_specs=[
            pl.BlockSpec((tm, d), row),
            pl.BlockSpec((1, 6, d), lambda i: (i // tiles_per_b, 0, 0)),
            pl.BlockSpec((1, d), lambda i: (0, 0)),
            pl.BlockSpec((n_a, d), lambda i: (0, 0)),
            pl.BlockSpec((d, 4 * d), lambda i: (0, 0)),
        ],
        out_specs=[pl.BlockSpec((tm, d), row)] * 5,
        out_shape=[jax.ShapeDtypeStruct((n, d), BF16)] * 3
        + [jax.ShapeDtypeStruct((n, d), F32), jax.ShapeDtypeStruct((n, d), BF16)],
        compiler_params=_params(("arbitrary",)),
        name="hgrn_in",
    )(x2, mod, g_mix.reshape(1, d), lb_logits, w_in_bf)


def _mid_rows(cum, half):
    c, d = cum.shape
    blk = 2 * half
    if blk >= SUBLANES:
        parts = [jnp.broadcast_to(cum[b * blk + half - 1:b * blk + half, :], (blk, d))
                 for b in range(c // blk)]
        return jnp.concatenate(parts, axis=0)
    sub = lax.broadcasted_iota(I32, (SUBLANES, 1), 0)
    parts = []
    for g in range(c // SUBLANES):
        out = None
        for j in range(SUBLANES // blk):
            r = g * SUBLANES + j * blk + half - 1
            cand = jnp.broadcast_to(cum[r:r + 1, :], (SUBLANES, d))
            out = cand if out is None else jnp.where(sub >= j * blk, cand, out)
        parts.append(out)
    return jnp.concatenate(parts, axis=0)


def _hgrn_rec_kernel(n_heads, x_ref, mod_ref, q_ref, k_ref, v_ref, lf_ref, gate_ref,
                     tri_ref, gout_ref, wout_ref, o_ref, state_ref, obuf_ref):
    c = HG_CHUNK
    hd = HG_HEAD_DIM
    ts = x_ref.shape[0]

    @pl.when(pl.program_id(1) == 0)
    def _():
        state_ref[...] = jnp.zeros_like(state_ref)

    t_idx = lax.broadcasted_iota(I32, (c, c), 0)
    s_idx = lax.broadcasted_iota(I32, (c, c), 1)
    r_idx = lax.broadcasted_iota(I32, (c, 1), 0)
    pair_masks = []
    right_rows = []
    for lvl in range(N_LEVELS):
        pair_masks.append(jnp.logical_and(t_idx > s_idx, ((t_idx ^ s_idx) >> lvl) == 1))
        right_rows.append(((r_idx >> lvl) & 1) == 1)
    diag_mask = t_idx == s_idx
    tri = tri_ref[...]
    gout = gout_ref[...]

    def chunk_body(ci, carry):
        r0 = pl.multiple_of(ci * c, c)
        rows = pl.ds(r0, c)
        lf = lf_ref[rows, :]
        qa = q_ref[rows, :].astype(F32)
        ka = k_ref[rows, :].astype(F32)
        va = v_ref[rows, :].astype(F32)
        l_hi = lf.astype(BF16)
        rem = lf - l_hi.astype(F32)
        l_mid = rem.astype(BF16)
        l_lo = (rem - l_mid.astype(F32)).astype(BF16)
        cum = _dot(tri, l_hi) + _dot(tri, l_mid) + _dot(tri, l_lo)
        e_cum = jnp.exp(cum)
        e_tail = jnp.exp(jnp.broadcast_to(cum[c - 1:c, :], cum.shape) - cum)
        q_in = (qa * e_cum).astype(BF16)
        k_out = (ka * e_tail).astype(BF16)
        vb = va.astype(BF16)
        qk_diag = qa * ka
        qk_sub = qa * jnp.exp(lf) * pltpu.roll(ka, 1, 0)
        xs = []
        for lvl in range(1, N_LEVELS):
            half = 1 << lvl
            if half >= SUBLANES:
                r_parts, x_parts = [], []
                for lo in range(0, c, 2 * half):
                    mid, hi = lo + half, lo + 2 * half
                    m_row = jnp.broadcast_to(cum[mid - 1:mid, :], (half, cum.shape[1]))
                    r_parts += [m_row - cum[lo:mid, :], cum[mid:hi, :] - m_row]
                    x_parts += [ka[lo:mid, :], qa[mid:hi, :]]
                e = jnp.exp(jnp.concatenate(r_parts, axis=0))
                xs.append((jnp.concatenate(x_parts, axis=0) * e).astype(BF16))
            else:
                e = jnp.exp(-jnp.abs(cum - _mid_rows(cum, half)))
                xs.append((jnp.where(right_rows[lvl], qa, ka) * e).astype(BF16))
        for h in range(n_heads):
            cols = slice(h * hd, (h + 1) * hd)
            d_col = jnp.sum(qk_diag[:, cols], axis=-1, keepdims=True)
            s_col = jnp.sum(qk_sub[:, cols], axis=-1, keepdims=True)
            sc = jnp.where(diag_mask, d_col, jnp.where(pair_masks[0], s_col, 0.0))
            for lvl in range(1, N_LEVELS):
                xl = xs[lvl - 1][:, cols]
                sc = jnp.where(pair_masks[lvl], _dot_nt(xl, xl), sc)
            st = state_ref[h]
            o = _dot_nt(q_in[:, cols], st.astype(BF16)) + _dot(sc.astype(BF16), vb[:, cols])
            upd = _dot(va[:, cols].T.astype(BF16), k_out[:, cols])
            state_ref[h] = st * e_cum[c - 1:c, cols] + upd
            on = _rms(o, gout) * gate_ref[rows, cols].astype(F32)
            obuf_ref[rows, cols] = on.astype(BF16)
        return carry

    lax.fori_loop(0, ts // c, chunk_body, 0)
    y = _dot(obuf_ref[...], wout_ref[...])
    o_ref[...] = x_ref[...] + mod_ref[0][2:3] * y


def _hgrn_rec(x2, mod, q, k, v, lf, gate, g_out, w_out_bf, bsz, seq):
    n, d = x2.shape
    ts = 512
    n_heads = d // HG_HEAD_DIM
    tiles = seq // ts
    row = lambda b, j: (b * tiles + j, 0)
    const = lambda b, j: (0, 0)
    tri = jnp.asarray(np.tril(np.ones((HG_CHUNK, HG_CHUNK), np.float32)), BF16)
    return pl.pallas_call(
        functools.partial(_hgrn_rec_kernel, n_heads),
        grid=(bsz, tiles),
        in_specs=[
            pl.BlockSpec((ts, d), row),
            pl.BlockSpec((1, 6, d), lambda b, j: (b, 0, 0)),
            pl.BlockSpec((ts, d), row), pl.BlockSpec((ts, d), row), pl.BlockSpec((ts, d), row),
            pl.BlockSpec((ts, d), row), pl.BlockSpec((ts, d), row),
            pl.BlockSpec(tri.shape, const),
            pl.BlockSpec((1, HG_HEAD_DIM), const),
            pl.BlockSpec((d, d), const),
        ],
        out_specs=pl.BlockSpec((ts, d), row),
        out_shape=jax.ShapeDtypeStruct((n, d), F32),
        scratch_shapes=[
            pltpu.VMEM((n_heads, HG_HEAD_DIM, HG_HEAD_DIM), F32),
            pltpu.VMEM((ts, d), BF16),
        ],
        compiler_params=_params(("arbitrary", "arbitrary")),
        name="hgrn_rec",
    )(x2, mod, q, k, v, lf, gate, tri, g_out.reshape(1, HG_HEAD_DIM), w_out_bf)


def _store_token_tiles(ref, val):
    tm = val.shape[0]
    for cidx in range(val.shape[1] // LANES):
        ref[pl.ds(cidx, tm, stride=SUBLANES), :] = val[:, cidx * LANES:(cidx + 1) * LANES]


def _load_token_tiles(ref, tm):
    n_c = ref.shape[0] // tm
    return jnp.concatenate([ref[pl.ds(cidx, tm, stride=SUBLANES), :] for cidx in range(n_c)], axis=-1)


def _router_kernel(x_ref, mod_ref, g_ref, wr_ref, br_ref, h_ref, idx_ref, wt_ref, cnt_ref):
    tm = x_ref.shape[0]
    mod = mod_ref[0]
    h = _norm_mod(x_ref[...], g_ref[...], mod[3:4], mod[4:5])
    _store_token_tiles(h_ref, h)
    logits = _dot3(h, wr_ref[...]) + br_ref[...]
    lane = lax.broadcasted_iota(I32, (tm, LANES), 1)
    lane_f = lane.astype(F32)
    idx_acc = jnp.zeros((tm, LANES), F32)
    val_acc = jnp.zeros((tm, LANES), F32)
    cnt = jnp.zeros((tm, LANES), F32)
    work = logits
    for kk in range(TOP_K):
        m = jnp.max(work, axis=-1, keepdims=True)
        sel = jnp.min(jnp.where(work == m, lane_f, float(LANES)), axis=-1, keepdims=True)
        hit = lane_f == sel
        work = jnp.where(hit, -jnp.inf, work)
        cnt = cnt + jnp.where(hit, 1.0, 0.0)
        idx_acc = jnp.where(lane == kk, sel, idx_acc)
        val_acc = jnp.where(lane == kk, m, val_acc)
    top = jnp.max(jnp.where(lane < TOP_K, val_acc, -jnp.inf), axis=-1, keepdims=True)
    ex = jnp.where(lane < TOP_K, jnp.exp(val_acc - top), 0.0)
    wts = ex / jnp.sum(ex, axis=-1, keepdims=True)
    idx_ref[...] = idx_acc[:, :TOP_K].astype(I32)
    wt_ref[...] = wts[:, :TOP_K]

    @pl.when(pl.program_id(0) == 0)
    def _():
        cnt_ref[...] = jnp.zeros_like(cnt_ref)

    cnt_ref[...] += jnp.sum(cnt, axis=0, keepdims=True)


def _moe_router(x2, mod, g_ffn, w_router, b_router, seq):
    n, d = x2.shape
    tm = 512
    tiles_per_b = seq // tm
    wr = jnp.zeros((d, LANES), F32).at[:, :N_EXPERTS].set(w_router)
    br = jnp.full((1, LANES), NEG_BIG, F32).at[0, :N_EXPERTS].set(b_router)
    return pl.pallas_call(
        _router_kernel,
        grid=(n // tm,),
        in_specs=[
            pl.BlockSpec((tm, d), lambda i: (i, 0)),
            pl.BlockSpec((1, 6, d), lambda i: (i // tiles_per_b, 0, 0)),
            pl.BlockSpec((1, d), lambda i: (0, 0)),
            pl.BlockSpec((d, LANES), lambda i: (0, 0)),
            pl.BlockSpec((1, LANES), lambda i: (0, 0)),
        ],
        out_specs=[
            pl.BlockSpec((tm * SUBLANES, LANES), lambda i: (i, 0)),
            pl.BlockSpec((tm, TOP_K), lambda i: (i, 0)),
            pl.BlockSpec((tm, TOP_K), lambda i: (i, 0)),
            pl.BlockSpec((1, LANES), lambda i: (0, 0)),
        ],
        out_shape=[
            jax.ShapeDtypeStruct((n * SUBLANES, LANES), F32),
            jax.ShapeDtypeStruct((n, TOP_K), I32),
            jax.ShapeDtypeStruct((n, TOP_K), F32),
            jax.ShapeDtypeStruct((1, LANES), F32),
        ],
        compiler_params=_params(("arbitrary",)),
        name="moe_router",
    )(x2, mod, g_ffn.reshape(1, d), wr, br)


def _dest_kernel(idx_ref, start_ref, tri_ref, dest_ref, carry_ref):
    tm = idx_ref.shape[0]

    @pl.when(pl.program_id(0) == 0)
    def _():
        carry_ref[...] = jnp.zeros_like(carry_ref)

    lane = lax.broadcasted_iota(I32, (tm, LANES), 1)
    idx = idx_ref[...]
    hits = [lane == idx[:, kk:kk + 1] for kk in range(TOP_K)]
    cnt = functools.reduce(lambda a, b: a + b, [jnp.where(hh, 1.0, 0.0) for hh in hits])
    before = _dot(tri_ref[...], cnt.astype(BF16))
    pos = before + carry_ref[...] + start_ref[...]
    acc = jnp.zeros((tm, LANES), F32)
    for kk in range(TOP_K):
        dk = jnp.sum(jnp.where(hits[kk], pos, 0.0), axis=-1, keepdims=True)
        acc = jnp.where(lane == kk, dk, acc)
    dest_ref[...] = acc[:, :TOP_K].astype(I32)
    carry_ref[...] += jnp.sum(cnt, axis=0, keepdims=True)


def _moe_dest(idx, pad_start):
    n = idx.shape[0]
    tm = 512
    tri = jnp.asarray(np.tril(np.ones((tm, tm), np.float32), -1), BF16)
    return pl.pallas_call(
        _dest_kernel,
        grid=(n // tm,),
        in_specs=[
            pl.BlockSpec((tm, TOP_K), lambda i: (i, 0)),
            pl.BlockSpec((1, LANES), lambda i: (0, 0)),
            pl.BlockSpec((tm, tm), lambda i: (0, 0)),
        ],
        out_specs=pl.BlockSpec((tm, TOP_K), lambda i: (i, 0)),
        out_shape=jax.ShapeDtypeStruct((n, TOP_K), I32),
        scratch_shapes=[pltpu.VMEM((1, LANES), F32)],
        compiler_params=_params(("arbitrary",)),
        name="moe_dest",
    )(idx, pad_start, tri)


DISPATCH_TOKENS = 256


def _token_copy(src, src_tok, dst, dst_tok, sem):
    s0 = pl.multiple_of(src_tok * SUBLANES, SUBLANES)
    d0 = pl.multiple_of(dst_tok * SUBLANES, SUBLANES)
    return pltpu.make_async_copy(src.at[pl.ds(s0, SUBLANES), :], dst.at[pl.ds(d0, SUBLANES), :], sem)


def _dispatch_kernel(gend_ref, gpad_ref, dest_ref, h_ref, xs_ref, zero_ref, sem, zsem):
    t = DISPATCH_TOKENS
    blk_rows = MOE_BLOCK * SUBLANES

    @pl.when(pl.program_id(0) == 0)
    def _():
        zero_ref[...] = jnp.zeros_like(zero_ref)

        def zero_copy(e):
            start = pl.multiple_of((gend_ref[e] - MOE_BLOCK) * SUBLANES, SUBLANES)
            return pltpu.make_async_copy(zero_ref, xs_ref.at[pl.ds(start, blk_rows), :], zsem)

        n_blocks = xs_ref.shape[0] // blk_rows
        first_free = gend_ref[N_EXPERTS - 1] // MOE_BLOCK

        def free_copy(j):
            start = pl.multiple_of((first_free + j) * blk_rows, blk_rows)
            return pltpu.make_async_copy(zero_ref, xs_ref.at[pl.ds(start, blk_rows), :], zsem)

        for e in range(N_EXPERTS):
            @pl.when(gpad_ref[e] > 0)
            def _():
                zero_copy(e).start()

            @pl.when(first_free + e < n_blocks)
            def _():
                free_copy(e).start()
        for e in range(N_EXPERTS):
            @pl.when(gpad_ref[e] > 0)
            def _():
                zero_copy(e).wait()

            @pl.when(first_free + e < n_blocks)
            def _():
                free_copy(e).wait()

    def start(tok, carry):
        for kk in range(TOP_K):
            _token_copy(h_ref, tok, xs_ref, dest_ref[tok * TOP_K + kk], sem).start(priority=kk % 2)
        return carry

    lax.fori_loop(0, t, start, 0, unroll=4)
    for _ in range(TOP_K):
        pltpu.make_async_copy(h_ref, xs_ref.at[pl.ds(0, t * SUBLANES), :], sem).wait()


def _moe_dispatch(h_tt, dest, pad_end, padded, n_rows):
    n = dest.shape[0]
    t = DISPATCH_TOKENS
    grid_spec = pltpu.PrefetchScalarGridSpec(
        num_scalar_prefetch=2,
        grid=(n // t,),
        in_specs=[
            pl.BlockSpec((t * TOP_K,), lambda i, ge, gp: (i,), memory_space=pltpu.SMEM),
            pl.BlockSpec((t * SUBLANES, LANES), lambda i, ge, gp: (i, 0)),
        ],
        out_specs=pl.BlockSpec(memory_space=pl.ANY),
        scratch_shapes=[
            pltpu.VMEM((MOE_BLOCK * SUBLANES, LANES), F32),
            pltpu.SemaphoreType.DMA(()),
            pltpu.SemaphoreType.DMA(()),
        ],
    )
    return pl.pallas_call(
        _dispatch_kernel,
        grid_spec=grid_spec,
        out_shape=jax.ShapeDtypeStruct((n_rows * SUBLANES, LANES), F32),
        compiler_params=_params(("arbitrary",)),
        name="moe_dispatch",
    )(pad_end, padded, dest.reshape(n * TOP_K), h_tt)


def _experts_kernel(be_ref, na_ref, xs_ref, wgu_ref, bgu_ref, wdn_ref, bdn_ref, ys_ref, wgu_bf, wdn_bf):
    i = pl.program_id(0)
    tm = MOE_BLOCK
    ff = wdn_ref.shape[2]

    @pl.when((i == 0) | (be_ref[i] != be_ref[jnp.maximum(i - 1, 0)]))
    def _():
        wgu_bf[...] = wgu_ref[0, 0].astype(BF16)
        wdn_bf[...] = wdn_ref[0, 0].astype(BF16)

    @pl.when(i < na_ref[0])
    def _():
        xb = _load_token_tiles(xs_ref, tm).astype(BF16)
        gu = _dot(xb, wgu_bf[...]) + bgu_ref[0, 0]
        gate = jnp.minimum(gu[:, :ff], SWIGLU_LIMIT)
        up = jnp.clip(gu[:, ff:], -SWIGLU_LIMIT, SWIGLU_LIMIT)
        glu = gate * _sigmoid(SWIGLU_ALPHA * gate)
        y = _dot(((up + 1.0) * glu).astype(BF16), wdn_bf[...]) + bdn_ref[0, 0]
        _store_token_tiles(ys_ref, y)

    @pl.when(i >= na_ref[0])
    def _():
        ys_ref[...] = jnp.zeros_like(ys_ref)


def _moe_experts(xs, block_e, n_active, layer, w_gu, b_gu, w_dn, b_dn):
    n_l, n_e, d, ff2 = w_gu.shape
    ff = ff2 // 2
    n_blocks = block_e.shape[0]
    tm = MOE_BLOCK
    grid_spec = pltpu.PrefetchScalarGridSpec(
        num_scalar_prefetch=2,
        grid=(n_blocks,),
        in_specs=[
            pl.BlockSpec((tm * SUBLANES, LANES), lambda i, be, na: (jnp.minimum(i, na[0] - 1), 0)),
            pl.BlockSpec((1, 1, d, ff2), lambda i, be, na: (layer, be[i], 0, 0)),
            pl.BlockSpec((1, 1, 1, ff2), lambda i, be, na: (layer, be[i], 0, 0)),
            pl.BlockSpec((1, 1, ff, d), lambda i, be, na: (layer, be[i], 0, 0)),
            pl.BlockSpec((1, 1, 1, d), lambda i, be, na: (layer, be[i], 0, 0)),
        ],
        out_specs=pl.BlockSpec((tm * SUBLANES, LANES), lambda i, be, na: (i, 0)),
        scratch_shapes=[pltpu.VMEM((d, ff2), BF16), pltpu.VMEM((ff, d), BF16)],
    )
    return pl.pallas_call(
        _experts_kernel,
        grid_spec=grid_spec,
        out_shape=jax.ShapeDtypeStruct(xs.shape, F32),
        compiler_params=_params(("arbitrary",), EXPERT_VMEM_LIMIT),
        name="moe_experts",
    )(block_e, n_active, xs, w_gu, b_gu.reshape(n_l, n_e, 1, ff2), w_dn, b_dn.reshape(n_l, n_e, 1, d))


def _combine_kernel(dest_ref, x_ref, mod_ref, wt_ref, ys_ref, o_ref, buf_ref, sem):
    t = DISPATCH_TOKENS
    slot_rows = t * SUBLANES

    def start(tok, carry):
        for kk in range(TOP_K):
            _token_copy(ys_ref, dest_ref[tok * TOP_K + kk], buf_ref, kk * t + tok, sem).start(priority=kk % 2)
        return carry

    lax.fori_loop(0, t, start, 0, unroll=4)
    pltpu.make_async_copy(ys_ref.at[pl.ds(0, TOP_K * slot_rows), :], buf_ref, sem).wait()

    wt = wt_ref[...]
    y = jnp.zeros(x_ref.shape, F32)
    for kk in range(TOP_K):
        y = y + _load_token_tiles(buf_ref.at[pl.ds(kk * slot_rows, slot_rows), :], t) * wt[:, kk:kk + 1]
    o_ref[...] = x_ref[...] + mod_ref[0][5:6] * y


def _moe_combine(x2, mod, wts, dest, ys, seq):
    n, d = x2.shape
    t = DISPATCH_TOKENS
    tiles_per_b = seq // t
    return pl.pallas_call(
        _combine_kernel,
        grid=(n // t,),
        in_specs=[
            pl.BlockSpec((t * TOP_K,), lambda i: (i,), memory_space=pltpu.SMEM),
            pl.BlockSpec((t, d), lambda i: (i, 0)),
            pl.BlockSpec((1, 6, d), lambda i: (i // tiles_per_b, 0, 0)),
            pl.BlockSpec((t, TOP_K), lambda i: (i, 0)),
            pl.BlockSpec(memory_space=pl.ANY),
        ],
        out_specs=pl.BlockSpec((t, d), lambda i: (i, 0)),
        out_shape=jax.ShapeDtypeStruct((n, d), F32),
        scratch_shapes=[
            pltpu.VMEM((TOP_K * t * SUBLANES, LANES), F32),
            pltpu.SemaphoreType.DMA(()),
        ],
        compiler_params=_params(("arbitrary",)),
        name="moe_combine",
    )(dest.reshape(n * TOP_K), x2, mod, wts, ys)


def _moe_layer(x2, mod, g_ffn, w_router, b_router, layer, w_gu, b_gu, w_dn, b_dn, seq):
    n = x2.shape[0]
    h_tt, idx, wts, counts = _moe_router(x2, mod, g_ffn, w_router, b_router, seq)
    counts = counts[0, :N_EXPERTS].astype(I32)
    padded = (counts + MOE_BLOCK - 1) // MOE_BLOCK * MOE_BLOCK
    pad_end = jnp.cumsum(padded)
    pad_start = pad_end - padded
    n_rows = n * TOP_K + N_EXPERTS * MOE_BLOCK
    n_blocks = n_rows // MOE_BLOCK
    blk_start = jnp.arange(n_blocks, dtype=I32) * MOE_BLOCK
    block_e = jnp.minimum(jnp.sum((pad_end[None, :] <= blk_start[:, None]).astype(I32), axis=1),
                          N_EXPERTS - 1)
    n_active = (pad_end[-1:] // MOE_BLOCK).astype(I32)
    start_vec = jnp.zeros((1, LANES), F32).at[0, :N_EXPERTS].set(pad_start.astype(F32))
    dest = _moe_dest(idx, start_vec)
    xs = _moe_dispatch(h_tt, dest, pad_end.astype(I32), padded.astype(I32), n_rows)
    ys = _moe_experts(xs, block_e, n_active, layer, w_gu, b_gu, w_dn, b_dn)
    return _moe_combine(x2, mod, wts, dest, ys, seq)


def _rope_kernel(pos_ref, inv_ref, cos_ref, sin_ref):
    ang = pos_ref[...] * inv_ref[...]
    cos_ref[...] = jnp.cos(ang)
    sin_ref[...] = jnp.sin(ang)


def _rope_table(positions):
    half = MLA_ROPE // 2
    n = positions.size
    rep = LANES // half
    inv = np.power(np.float32(ROPE_THETA),
                   -np.arange(half, dtype=np.float32) * np.float32(2.0 / MLA_ROPE)).astype(np.float32)
    inv_row = jnp.asarray(np.tile(inv, rep).reshape(1, LANES))
    pos = jnp.repeat(positions.reshape(n).astype(F32), half).reshape(n // rep, LANES)
    rows = n // rep
    tr = min(rows, 2048)
    cos, sin = pl.pallas_call(
        _rope_kernel,
        grid=(rows // tr,),
        in_specs=[pl.BlockSpec((tr, LANES), lambda i: (i, 0)), pl.BlockSpec((1, LANES), lambda i: (0, 0))],
        out_specs=[pl.BlockSpec((tr, LANES), lambda i: (i, 0))] * 2,
        out_shape=[jax.ShapeDtypeStruct((rows, LANES), F32)] * 2,
        compiler_params=_params(("arbitrary",)),
        name="rope_table",
    )(pos, inv_row)
    return cos.reshape(n, half), sin.reshape(n, half)


def _rope_halves(t1, t2, cos, sin):
    return t1 * cos - t2 * sin, t2 * cos + t1 * sin


def _kv_kernel(n_heads, x_ref, mod_ref, g_ref, wd_ref, gc_ref, wu_ref, cos_ref, sin_ref, k_ref, v_ref):
    half = MLA_ROPE // 2
    mod = mod_ref[0]
    hk = _norm_mod(x_ref[...], g_ref[...], mod[0:1], mod[1:2]).astype(BF16)
    ckr = _dot(hk, wd_ref[...])
    c_kv = _rms(ckr[:, :MLA_KV_RANK], gc_ref[...]).astype(BF16)
    r1, r2 = _rope_halves(ckr[:, MLA_KV_RANK:MLA_KV_RANK + half], ckr[:, MLA_KV_RANK + half:],
                          cos_ref[...], sin_ref[...])
    kv = _dot(c_kv, wu_ref[...])
    nv = n_heads * MLA_NOPE
    for h in range(n_heads):
        k_ref[0, h] = jnp.concatenate(
            [kv[:, h * MLA_NOPE:(h + 1) * MLA_NOPE], r1, r2], axis=-1).astype(BF16)
        v_ref[0, h] = kv[:, nv + h * MLA_V:nv + (h + 1) * MLA_V].astype(BF16)


def _kv_shared(x2, mod_kv, g_kv, w_dkv, g_ckv, w_ukv, cos, sin, bsz, seq):
    n, d = x2.shape
    n_heads = w_ukv.shape[1] // (MLA_NOPE + MLA_V)
    half = MLA_ROPE // 2
    tm = 512
    tiles = seq // tm
    w3 = w_ukv.reshape(MLA_KV_RANK, n_heads, MLA_NOPE + MLA_V)
    wu = jnp.concatenate([w3[:, :, :MLA_NOPE].reshape(MLA_KV_RANK, -1),
                          w3[:, :, MLA_NOPE:].reshape(MLA_KV_RANK, -1)], axis=1).astype(BF16)
    row = lambda b, j: (b * tiles + j, 0)
    const = lambda b, j: (0, 0)
    qk = MLA_NOPE + MLA_ROPE
    return pl.pallas_call(
        functools.partial(_kv_kernel, n_heads),
        grid=(bsz, tiles),
        in_specs=[
            pl.BlockSpec((tm, d), row),
            pl.BlockSpec((1, 2, d), lambda b, j: (b, 0, 0)),
            pl.BlockSpec((1, d), const),
            pl.BlockSpec(w_dkv.shape, const),
            pl.BlockSpec((1, MLA_KV_RANK), const),
            pl.BlockSpec(wu.shape, const),
            pl.BlockSpec((tm, half), row),
            pl.BlockSpec((tm, half), row),
        ],
        out_specs=[
            pl.BlockSpec((1, n_heads, tm, qk), lambda b, j: (b, 0, j, 0)),
            pl.BlockSpec((1, n_heads, tm, MLA_V), lambda b, j: (b, 0, j, 0)),
        ],
        out_shape=[
            jax.ShapeDtypeStruct((bsz, n_heads, seq, qk), BF16),
            jax.ShapeDtypeStruct((bsz, n_heads, seq, MLA_V), BF16),
        ],
        compiler_params=_params(("arbitrary", "arbitrary")),
        name="kv_shared",
    )(x2, mod_kv, g_kv.reshape(1, d), w_dkv.astype(BF16), g_ckv.reshape(1, MLA_KV_RANK), wu, cos, sin)


def _mla_q_kernel(n_heads, x_ref, mod_ref, g_ref, wd_ref, gc_ref, wu_ref, cos_ref, sin_ref, q_ref):
    half = MLA_ROPE // 2
    mod = mod_ref[0]
    h = _norm_mod(x_ref[...], g_ref[...], mod[0:1], mod[1:2]).astype(BF16)
    cq = _rms(_dot(h, wd_ref[...]), gc_ref[...]).astype(BF16)
    q = _dot(cq, wu_ref[...])
    n1 = n_heads * MLA_NOPE
    n2 = n1 + n_heads * half
    cos = jnp.concatenate([cos_ref[...]] * n_heads, axis=-1)
    sin = jnp.concatenate([sin_ref[...]] * n_heads, axis=-1)
    r1, r2 = _rope_halves(q[:, n1:n2], q[:, n2:], cos, sin)
    scale = math.log2(math.e) / math.sqrt(MLA_NOPE + MLA_ROPE)
    for hh in range(n_heads):
        q_ref[0, hh] = (jnp.concatenate(
            [q[:, hh * MLA_NOPE:(hh + 1) * MLA_NOPE],
             r1[:, hh * half:(hh + 1) * half], r2[:, hh * half:(hh + 1) * half]], axis=-1) * scale).astype(BF16)


def _mla_q(x2, mod, g_mix, w_dq, g_cq, w_uq, cos, sin, bsz, seq):
    n, d = x2.shape
    q_rank = w_dq.shape[1]
    qk = MLA_NOPE + MLA_ROPE
    n_heads = w_uq.shape[1] // qk
    half = MLA_ROPE // 2
    tm = 512
    tiles = seq // tm
    w3 = w_uq.reshape(q_rank, n_heads, qk)
    wu = jnp.concatenate([w3[:, :, :MLA_NOPE].reshape(q_rank, -1),
                          w3[:, :, MLA_NOPE:MLA_NOPE + half].reshape(q_rank, -1),
                          w3[:, :, MLA_NOPE + half:].reshape(q_rank, -1)], axis=1).astype(BF16)
    row = lambda b, j: (b * tiles + j, 0)
    const = lambda b, j: (0, 0)
    return pl.pallas_call(
        functools.partial(_mla_q_kernel, n_heads),
        grid=(bsz, tiles),
        in_specs=[
            pl.BlockSpec((tm, d), row),
            pl.BlockSpec((1, 6, d), lambda b, j: (b, 0, 0)),
            pl.BlockSpec((1, d), const),
            pl.BlockSpec(w_dq.shape, const),
            pl.BlockSpec((1, q_rank), const),
            pl.BlockSpec(wu.shape, const),
            pl.BlockSpec((tm, half), row),
            pl.BlockSpec((tm, half), row),
        ],
        out_specs=pl.BlockSpec((1, n_heads, tm, qk), lambda b, j: (b, 0, j, 0)),
        out_shape=jax.ShapeDtypeStruct((bsz, n_heads, seq, qk), BF16),
        compiler_params=_params(("arbitrary", "arbitrary")),
        name="mla_q",
    )(x2, mod, g_mix.reshape(1, d), w_dq.astype(BF16), g_cq.reshape(1, q_rank), wu, cos, sin)


ATTN_BLOCK = 512
ATTN_SUB_BLOCKS = 1


def _flash_kernel(q_ref, k_ref, v_ref, o_ref):
    t = ATTN_BLOCK
    n_sub = ATTN_SUB_BLOCKS
    ts = t // n_sub
    qi = pl.program_id(2)
    dv = v_ref.shape[3]
    q_subs = [q_ref[0, 0, i * ts:(i + 1) * ts, :] for i in range(n_sub)]

    def block(carry, start, width, diagonal):
        k = k_ref[0, 0, pl.ds(start, width), :]
        v = v_ref[0, 0, pl.ds(start, width), :]
        out = []
        for i, (m_prev, l_prev, acc) in enumerate(carry):
            s = _dot_nt(q_subs[i], k)
            if diagonal:
                row = lax.broadcasted_iota(I32, (ts, width), 0) + i * ts
                col = lax.broadcasted_iota(I32, (ts, width), 1)
                s = jnp.where(col <= row, s, NEG_BIG)
            m_new = jnp.maximum(m_prev, jnp.max(s, axis=-1, keepdims=True))
            alpha = jnp.exp2(m_prev - m_new)
            pr = jnp.exp2(s - m_new)
            l_new = alpha * l_prev + jnp.sum(pr, axis=-1, keepdims=True)
            out.append((m_new, l_new, alpha * acc + _dot(pr.astype(BF16), v)))
        return tuple(out)

    init = tuple((jnp.full((ts, 1), NEG_BIG, F32), jnp.zeros((ts, 1), F32), jnp.zeros((ts, dv), F32))
                 for _ in range(n_sub))
    carry = lax.fori_loop(
        0, lax.shift_right_logical(qi, 1),
        lambda jj, c: block(c, pl.multiple_of(jj * (2 * t), 2 * t), 2 * t, False), init)
    carry = lax.cond(
        (qi & 1) == 1,
        lambda c: block(c, pl.multiple_of((qi - 1) * t, t), t, False),
        lambda c: c, carry)
    carry = block(carry, pl.multiple_of(qi * t, t), t, True)
    for i, (_, l_fin, acc) in enumerate(carry):
        o_ref[0, i * ts:(i + 1) * ts, :] = (acc / l_fin).astype(BF16)


def _flash_attn(q, k, v):
    bsz, n_heads, seq, qk = q.shape
    dv = v.shape[3]
    t = ATTN_BLOCK
    return pl.pallas_call(
        _flash_kernel,
        grid=(bsz, n_heads, seq // t),
        in_specs=[
            pl.BlockSpec((1, 1, t, qk), lambda b, h, i: (b, h, i, 0)),
            pl.BlockSpec((1, 1, seq, qk), lambda b, h, i: (b, h, 0, 0)),
            pl.BlockSpec((1, 1, seq, dv), lambda b, h, i: (b, h, 0, 0)),
        ],
        out_specs=pl.BlockSpec((1, t, dv), lambda b, h, i: (b, i, h)),
        out_shape=jax.ShapeDtypeStruct((bsz, seq, n_heads * dv), BF16),
        compiler_params=_params(("arbitrary", "arbitrary", "arbitrary")),
        name="flash_attn",
    )(q, k, v)


def _attn_out_kernel(x_ref, mod_ref, o_ref, w_ref, y_ref):
    y_ref[...] = x_ref[...] + mod_ref[0][2:3] * _dot(o_ref[...], w_ref[...])


def _attn_out(x2, mod, o2, w_o_bf, seq):
    n, d = x2.shape
    tm = 512
    tiles_per_b = seq // tm
    return pl.pallas_call(
        _attn_out_kernel,
        grid=(n // tm,),
        in_specs=[
            pl.BlockSpec((tm, d), lambda i: (i, 0)),
            pl.BlockSpec((1, 6, d), lambda i: (i // tiles_per_b, 0, 0)),
            pl.BlockSpec((tm, o2.shape[1]), lambda i: (i, 0)),
            pl.BlockSpec(w_o_bf.shape, lambda i: (0, 0)),
        ],
        out_specs=pl.BlockSpec((tm, d), lambda i: (i, 0)),
        out_shape=jax.ShapeDtypeStruct((n, d), F32),
        compiler_params=_params(("arbitrary",)),
        name="attn_out",
    )(x2, mod, o2, w_o_bf)


def _final_kernel(x_ref, g_ref, o_ref):
    o_ref[...] = _rms(x_ref[...], g_ref[...])


def _final_norm(x2, g):
    n, d = x2.shape
    tm = 1024
    return pl.pallas_call(
        _final_kernel,
        grid=(n // tm,),
        in_specs=[pl.BlockSpec((tm, d), lambda i: (i, 0)), pl.BlockSpec((1, d), lambda i: (0, 0))],
        out_specs=pl.BlockSpec((tm, d), lambda i: (i, 0)),
        out_shape=jax.ShapeDtypeStruct((n, d), F32),
        compiler_params=_params(("arbitrary",)),
        name="final_norm",
    )(x2, g.reshape(1, d))


def kernel(x, c, positions, g_mix, g_ffn, w_ada, b_ada, w_in_a, lb_logits, g_out_a, w_out_a, g_kv, w_ada_kv, b_ada_kv, w_dkv, g_ckv, w_ukv, w_dq, g_cq, w_uq, w_o_b, w_router, b_router, w_gu, b_gu, w_dn, b_dn, g_final):
    bsz, seq, d = x.shape
    depth = g_mix.shape[0]
    n_a = w_in_a.shape[0]
    n = bsz * seq
    x2 = x.reshape(n, d)

    mods = _ada_mod(c, w_ada, b_ada)
    mods = mods.reshape(depth, bsz, 6, d)
    mod_kv = _ada_mod(c, w_ada_kv[None], b_ada_kv[None]).reshape(bsz, 2, d)

    shared = None
    cos = sin = None
    for l in range(depth):
        mod = mods[l]
        if l < n_a:
            q, k, v, lf, gate = _hgrn_in(x2, mod, g_mix[l], lb_logits, w_in_a[l].astype(BF16), l, seq)
            x2 = _hgrn_rec(x2, mod, q, k, v, lf, gate, g_out_a[l], w_out_a[l].astype(BF16), bsz, seq)
        else:
            j = l - n_a
            qh = _mla_q(x2, mod, g_mix[l], w_dq[j], g_cq[j], w_uq[j], cos, sin, bsz, seq)
            o2 = _flash_attn(qh, *shared)
            x2 = _attn_out(x2, mod, o2.reshape(n, -1), w_o_b[j].astype(BF16), seq)
        x2 = _moe_layer(x2, mod, g_ffn[l], w_router[l], b_router[l],
                        l, w_gu, b_gu, w_dn, b_dn, seq)
        if l == n_a - 1:
            cos, sin = _rope_table(positions)
            shared = _kv_shared(x2, mod_kv, g_kv, w_dkv, g_ckv, w_ukv, cos, sin, bsz, seq)
    return _final_norm(x2, g_final).reshape(bsz, seq, d)
```

```python
import functools
import math

import numpy as np
import jax
import jax.numpy as jnp
from jax import lax
from jax.experimental import pallas as pl
from jax.experimental.pallas import tpu as pltpu

F32 = jnp.float32
BF16 = jnp.bfloat16
I32 = jnp.int32

LANES = 128
SUBLANES = 8
VMEM_LIMIT = 48 * 1024 * 1024
EXPERT_VMEM_LIMIT = 56 * 1024 * 1024

HG_HEAD_DIM = 128
HG_CHUNK = 256
FORGET_FLOOR = 1e-30
MLA_NOPE = 128
MLA_ROPE = 64
MLA_V = 128
MLA_KV_RANK = 256
ROPE_THETA = 10000.0
N_EXPERTS = 32
TOP_K = 4
SWIGLU_LIMIT = 7.0
SWIGLU_ALPHA = 1.702
MOE_BLOCK = 512
NORM_EPS = 1e-6
NEG_BIG = -1e30

N_LEVELS = int(math.log2(HG_CHUNK))


def _params(sem, vmem=VMEM_LIMIT):
    return pltpu.CompilerParams(dimension_semantics=sem, vmem_limit_bytes=vmem)


def _dot(a, b):
    return jnp.dot(a, b, preferred_element_type=F32)


def _dot_nt(a, b):
    return lax.dot_general(a, b, (((1,), (1,)), ((), ())), preferred_element_type=F32)


def _split2(a):
    hi = a.astype(BF16)
    lo = (a - hi.astype(F32)).astype(BF16)
    return hi, lo


def _dot3(a, b):
    a_hi, a_lo = _split2(a)
    b_hi, b_lo = _split2(b)
    return _dot(a_hi, b_hi) + _dot(a_lo, b_hi) + _dot(a_hi, b_lo)


def _sigmoid(x):
    return 1.0 / (1.0 + jnp.exp(-x))


def _rms(x, gain):
    ms = jnp.mean(x * x, axis=-1, keepdims=True)
    return x * lax.rsqrt(ms + NORM_EPS) * gain


def _norm_mod(x, gain, shift, scale):
    return _rms(x, gain) * (1.0 + scale) + shift


def _ada_kernel(c_ref, w_ref, b_ref, o_ref):
    c = c_ref[...]
    c_act = c * _sigmoid(c)
    o_ref[0] = _dot3(c_act, w_ref[0]) + b_ref[0]


def _ada_mod(c, w, b):
    n_l, d, m = w.shape
    bsz = c.shape[0]
    tn = 1024
    return pl.pallas_call(
        _ada_kernel,
        grid=(n_l, m // tn),
        in_specs=[
            pl.BlockSpec((bsz, d), lambda l, j: (0, 0)),
            pl.BlockSpec((1, d, tn), lambda l, j: (l, 0, j)),
            pl.BlockSpec((1, 1, tn), lambda l, j: (l, 0, j)),
        ],
        out_specs=pl.BlockSpec((1, bsz, tn), lambda l, j: (l, 0, j)),
        out_shape=jax.ShapeDtypeStruct((n_l, bsz, m), F32),
        compiler_params=_params(("arbitrary", "arbitrary")),
        name="ada_mod",
    )(c, w, b.reshape(n_l, 1, m))


def _hgrn_in_kernel(layer, n_a, x_ref, mod_ref, g_ref, lbl_ref, w_ref,
                    q_ref, k_ref, v_ref, lf_ref, gate_ref):
    d = x_ref.shape[1]
    mod = mod_ref[0]
    h = _norm_mod(x_ref[...], g_ref[...], mod[0:1], mod[1:2]).astype(BF16)
    rows = [lbl_ref[i:i + 1, :] for i in range(n_a)]
    mx = functools.reduce(jnp.maximum, rows)
    ex = [jnp.exp(r - mx) for r in rows]
    den = functools.reduce(lambda a, b: a + b, ex)
    lb = (functools.reduce(lambda a, b: a + b, ex[:layer + 1]) - ex[0]) / den

    yq = _dot(h, w_ref[:, 0:d])
    q_ref[...] = (yq * _sigmoid(yq)).astype(BF16)
    yf = _dot(h, w_ref[:, d:2 * d])
    sg = _sigmoid(yf)
    forget = lb + (1.0 - lb) * sg
    lf_ref[...] = jnp.log(jnp.maximum(forget, FORGET_FLOOR))
    k_ref[...] = ((1.0 - lb) * (1.0 - sg)).astype(BF16)
    v_ref[...] = _dot(h, w_ref[:, 2 * d:3 * d]).astype(BF16)
    yg = _dot(h, w_ref[:, 3 * d:4 * d])
    gate_ref[...] = (yg * _sigmoid(yg)).astype(BF16)


def _hgrn_in(x2, mod, g_mix, lb_logits, w_in_bf, layer, seq):
    n, d = x2.shape
    tm = 512
    n_a = lb_logits.shape[0]
    tiles_per_b = seq // tm
    row = lambda i: (i, 0)
    return pl.pallas_call(
        functools.partial(_hgrn_in_kernel, layer, n_a),
        grid=(n // tm,),
        in_specs=[
            pl.BlockSpec((tm, d), row),
            pl.BlockSpec((1, 6, d), lambda i: (i // tiles_per_b, 0, 0)),
            pl.BlockSpec((1, d), lambda i: (0, 0)),
            pl.BlockSpec((n_a, d), lambda i: (0, 0)),
            pl.BlockSpec((d, 4 * d), lambda i: (0, 0)),
        ],
        out_specs=[pl.BlockSpec((tm, d), row)] * 5,
        out_shape=[jax.ShapeDtypeStruct((n, d), BF16)] * 3
        + [jax.ShapeDtypeStruct((n, d), F32), jax.ShapeDtypeStruct((n, d), BF16)],
        compiler_params=_params(("arbitrary",)),
        name="hgrn_in",
    )(x2, mod, g_mix.reshape(1, d), lb_logits, w_in_bf)


def _mid_rows(cum, half):
    c, d = cum.shape
    blk = 2 * half
    if blk >= SUBLANES:
        parts = [jnp.broadcast_to(cum[b * blk + half - 1:b * blk + half, :], (blk, d))
                 for b in range(c // blk)]
        return jnp.concatenate(parts, axis=0)
    sub = lax.broadcasted_iota(I32, (SUBLANES, 1), 0)
    parts = []
    for g in range(c // SUBLANES):
        out = None
        for j in range(SUBLANES // blk):
            r = g * SUBLANES + j * blk + half - 1
            cand = jnp.broadcast_to(cum[r:r + 1, :], (SUBLANES, d))
            out = cand if out is None else jnp.where(sub >= j * blk, cand, out)
        parts.append(out)
    return jnp.concatenate(parts, axis=0)


def _hgrn_rec_kernel(n_heads, x_ref, mod_ref, q_ref, k_ref, v_ref, lf_ref, gate_ref,
                     tri_ref, gout_ref, wout_ref, o_ref, state_ref, obuf_ref):
    c = HG_CHUNK
    hd = HG_HEAD_DIM
    ts = x_ref.shape[0]

    @pl.when(pl.program_id(1) == 0)
    def _():
        state_ref[...] = jnp.zeros_like(state_ref)

    t_idx = lax.broadcasted_iota(I32, (c, c), 0)
    s_idx = lax.broadcasted_iota(I32, (c, c), 1)
    r_idx = lax.broadcasted_iota(I32, (c, 1), 0)
    pair_masks = []
    right_rows = []
    for lvl in range(N_LEVELS):
        pair_masks.append(jnp.logical_and(t_idx > s_idx, ((t_idx ^ s_idx) >> lvl) == 1))
        right_rows.append(((r_idx >> lvl) & 1) == 1)
    diag_mask = t_idx == s_idx
    tri = tri_ref[...]
    gout = gout_ref[...]

    def chunk_body(ci, carry):
        r0 = pl.multiple_of(ci * c, c)
        rows = pl.ds(r0, c)
        lf = lf_ref[rows, :]
        qa = q_ref[rows, :].astype(F32)
        ka = k_ref[rows, :].astype(F32)
        va = v_ref[rows, :].astype(F32)
        l_hi = lf.astype(BF16)
        rem = lf - l_hi.astype(F32)
        l_mid = rem.astype(BF16)
        l_lo = (rem - l_mid.astype(F32)).astype(BF16)
        cum = _dot(tri, l_hi) + _dot(tri, l_mid) + _dot(tri, l_lo)
        e_cum = jnp.exp(cum)
        e_tail = jnp.exp(jnp.broadcast_to(cum[c - 1:c, :], cum.shape) - cum)
        q_in = (qa * e_cum).astype(BF16)
        k_out = (ka * e_tail).astype(BF16)
        vb = va.astype(BF16)
        qk_diag = qa * ka
        qk_sub = qa * jnp.exp(lf) * pltpu.roll(ka, 1, 0)
        xs = []
        for lvl in range(1, N_LEVELS):
            half = 1 << lvl
            if half >= SUBLANES:
                r_parts, x_parts = [], []
                for lo in range(0, c, 2 * half):
                    mid, hi = lo + half, lo + 2 * half
                    m_row = jnp.broadcast_to(cum[mid - 1:mid, :], (half, cum.shape[1]))
                    r_parts += [m_row - cum[lo:mid, :], cum[mid:hi, :] - m_row]
                    x_parts += [ka[lo:mid, :], qa[mid:hi, :]]
                e = jnp.exp(jnp.concatenate(r_parts, axis=0))
                xs.append((jnp.concatenate(x_parts, axis=0) * e).astype(BF16))
            else:
                e = jnp.exp(-jnp.abs(cum - _mid_rows(cum, half)))
                xs.append((jnp.where(right_rows[lvl], qa, ka) * e).astype(BF16))
        for h in range(n_heads):
            cols = slice(h * hd, (h + 1) * hd)
            d_col = jnp.sum(qk_diag[:, cols], axis=-1, keepdims=True)
            s_col = jnp.sum(qk_sub[:, cols], axis=-1, keepdims=True)
            sc = jnp.where(diag_mask, d_col, jnp.where(pair_masks[0], s_col, 0.0))
            for lvl in range(1, N_LEVELS):
                xl = xs[lvl - 1][:, cols]
                sc = jnp.where(pair_masks[lvl], _dot_nt(xl, xl), sc)
            st = state_ref[h]
            o = _dot_nt(q_in[:, cols], st.astype(BF16)) + _dot(sc.astype(BF16), vb[:, cols])
            upd = _dot(va[:, cols].T.astype(BF16), k_out[:, cols])
            state_ref[h] = st * e_cum[c - 1:c, cols] + upd
            on = _rms(o, gout) * gate_ref[rows, cols].astype(F32)
            obuf_ref[rows, cols] = on.astype(BF16)
        return carry

    lax.fori_loop(0, ts // c, chunk_body, 0)
    y = _dot(obuf_ref[...], wout_ref[...])
    o_ref[...] = x_ref[...] + mod_ref[0][2:3] * y


def _hgrn_rec(x2, mod, q, k, v, lf, gate, g_out, w_out_bf, bsz, seq):
    n, d = x2.shape
    ts = 512
    n_heads = d // HG_HEAD_DIM
    tiles = seq // ts
    row = lambda b, j: (b * tiles + j, 0)
    const = lambda b, j: (0, 0)
    tri = jnp.asarray(np.tril(np.ones((HG_CHUNK, HG_CHUNK), np.float32)), BF16)
    return pl.pallas_call(
        functools.partial(_hgrn_rec_kernel, n_heads),
        grid=(bsz, tiles),
        in_specs=[
            pl.BlockSpec((ts, d), row),
            pl.BlockSpec((1, 6, d), lambda b, j: (b, 0, 0)),
            pl.BlockSpec((ts, d), row), pl.BlockSpec((ts, d), row), pl.BlockSpec((ts, d), row),
            pl.BlockSpec((ts, d), row), pl.BlockSpec((ts, d), row),
            pl.BlockSpec(tri.shape, const),
            pl.BlockSpec((1, HG_HEAD_DIM), const),
            pl.BlockSpec((d, d), const),
        ],
        out_specs=pl.BlockSpec((ts, d), row),
        out_shape=jax.ShapeDtypeStruct((n, d), F32),
        scratch_shapes=[
            pltpu.VMEM((n_heads, HG_HEAD_DIM, HG_HEAD_DIM), F32),
            pltpu.VMEM((ts, d), BF16),
        ],
        compiler_params=_params(("arbitrary", "arbitrary")),
        name="hgrn_rec",
    )(x2, mod, q, k, v, lf, gate, tri, g_out.reshape(1, HG_HEAD_DIM), w_out_bf)


def _store_token_tiles(ref, val):
    tm = val.shape[0]
    for cidx in range(val.shape[1] // LANES):
        ref[pl.ds(cidx, tm, stride=SUBLANES), :] = val[:, cidx * LANES:(cidx + 1) * LANES]


def _load_token_tiles(ref, tm):
    n_c = ref.shape[0] // tm
    return jnp.concatenate([ref[pl.ds(cidx, tm, stride=SUBLANES), :] for cidx in range(n_c)], axis=-1)


def _router_kernel(x_ref, mod_ref, g_ref, wr_ref, br_ref, h_ref, idx_ref, wt_ref, cnt_ref):
    tm = x_ref.shape[0]
    mod = mod_ref[0]
    h = _norm_mod(x_ref[...], g_ref[...], mod[3:4], mod[4:5])
    _store_token_tiles(h_ref, h)
    logits = _dot3(h, wr_ref[...]) + br_ref[...]
    lane = lax.broadcasted_iota(I32, (tm, LANES), 1)
    lane_f = lane.astype(F32)
    idx_acc = jnp.zeros((tm, LANES), F32)
    val_acc = jnp.zeros((tm, LANES), F32)
    cnt = jnp.zeros((tm, LANES), F32)
    work = logits
    for kk in range(TOP_K):
        m = jnp.max(work, axis=-1, keepdims=True)
        sel = jnp.min(jnp.where(work == m, lane_f, float(LANES)), axis=-1, keepdims=True)
        hit = lane_f == sel
        work = jnp.where(hit, -jnp.inf, work)
        cnt = cnt + jnp.where(hit, 1.0, 0.0)
        idx_acc = jnp.where(lane == kk, sel, idx_acc)
        val_acc = jnp.where(lane == kk, m, val_acc)
    top = jnp.max(jnp.where(lane < TOP_K, val_acc, -jnp.inf), axis=-1, keepdims=True)
    ex = jnp.where(lane < TOP_K, jnp.exp(val_acc - top), 0.0)
    wts = ex / jnp.sum(ex, axis=-1, keepdims=True)
    idx_ref[...] = idx_acc[:, :TOP_K].astype(I32)
    wt_ref[...] = wts[:, :TOP_K]

    @pl.when(pl.program_id(0) == 0)
    def _():
        cnt_ref[...] = jnp.zeros_like(cnt_ref)

    cnt_ref[...] += jnp.sum(cnt, axis=0, keepdims=True)


def _moe_router(x2, mod, g_ffn, w_router, b_router, seq):
    n, d = x2.shape
    tm = 512
    tiles_per_b = seq // tm
    wr = jnp.zeros((d, LANES), F32).at[:, :N_EXPERTS].set(w_router)
    br = jnp.full((1, LANES), NEG_BIG, F32).at[0, :N_EXPERTS].set(b_router)
    return pl.pallas_call(
        _router_kernel,
        grid=(n // tm,),
        in_specs=[
            pl.BlockSpec((tm, d), lambda i: (i, 0)),
            pl.BlockSpec((1, 6, d), lambda i: (i // tiles_per_b, 0, 0)),
            pl.BlockSpec((1, d), lambda i: (0, 0)),
            pl.BlockSpec((d, LANES), lambda i: (0, 0)),
            pl.BlockSpec((1, LANES), lambda i: (0, 0)),
        ],
        out_specs=[
            pl.BlockSpec((tm * SUBLANES, LANES), lambda i: (i, 0)),
            pl.BlockSpec((tm, TOP_K), lambda i: (i, 0)),
            pl.BlockSpec((tm, TOP_K), lambda i: (i, 0)),
            pl.BlockSpec((1, LANES), lambda i: (0, 0)),
        ],
        out_shape=[
            jax.ShapeDtypeStruct((n * SUBLANES, LANES), F32),
            jax.ShapeDtypeStruct((n, TOP_K), I32),
            jax.ShapeDtypeStruct((n, TOP_K), F32),
            jax.ShapeDtypeStruct((1, LANES), F32),
        ],
        compiler_params=_params(("arbitrary",)),
        name="moe_router",
    )(x2, mod, g_ffn.reshape(1, d), wr, br)


def _dest_kernel(idx_ref, start_ref, tri_ref, dest_ref, carry_ref):
    tm = idx_ref.shape[0]

    @pl.when(pl.program_id(0) == 0)
    def _():
        carry_ref[...] = jnp.zeros_like(carry_ref)

    lane = lax.broadcasted_iota(I32, (tm, LANES), 1)
    idx = idx_ref[...]
    hits = [lane == idx[:, kk:kk + 1] for kk in range(TOP_K)]
    cnt = functools.reduce(lambda a, b: a + b, [jnp.where(hh, 1.0, 0.0) for hh in hits])
    before = _dot(tri_ref[...], cnt.astype(BF16))
    pos = before + carry_ref[...] + start_ref[...]
    acc = jnp.zeros((tm, LANES), F32)
    for kk in range(TOP_K):
        dk = jnp.sum(jnp.where(hits[kk], pos, 0.0), axis=-1, keepdims=True)
        acc = jnp.where(lane == kk, dk, acc)
    dest_ref[...] = acc[:, :TOP_K].astype(I32)
    carry_ref[...] += jnp.sum(cnt, axis=0, keepdims=True)


def _moe_dest(idx, pad_start):
    n = idx.shape[0]
    tm = 512
    tri = jnp.asarray(np.tril(np.ones((tm, tm), np.float32), -1), BF16)
    return pl.pallas_call(
        _dest_kernel,
        grid=(n // tm,),
        in_specs=[
            pl.BlockSpec((tm, TOP_K), lambda i: (i, 0)),
            pl.BlockSpec((1, LANES), lambda i: (0, 0)),
            pl.BlockSpec((tm, tm), lambda i: (0, 0)),
        ],
        out_specs=pl.BlockSpec((tm, TOP_K), lambda i: (i, 0)),
        out_shape=jax.ShapeDtypeStruct((n, TOP_K), I32),
        scratch_shapes=[pltpu.VMEM((1, LANES), F32)],
        compiler_params=_params(("arbitrary",)),
        name="moe_dest",
    )(idx, pad_start, tri)


DISPATCH_TOKENS = 256


def _token_copy(src, src_tok, dst, dst_tok, sem):
    s0 = pl.multiple_of(src_tok * SUBLANES, SUBLANES)
    d0 = pl.multiple_of(dst_tok * SUBLANES, SUBLANES)
    return pltpu.make_async_copy(src.at[pl.ds(s0, SUBLANES), :], dst.at[pl.ds(d0, SUBLANES), :], sem)


def _dispatch_kernel(gend_ref, gpad_ref, dest_ref, h_ref, xs_ref, zero_ref, sem, zsem):
    t = DISPATCH_TOKENS
    blk_rows = MOE_BLOCK * SUBLANES

    @pl.when(pl.program_id(0) == 0)
    def _():
        zero_ref[...] = jnp.zeros_like(zero_ref)

        def zero_copy(e):
            start = pl.multiple_of((gend_ref[e] - MOE_BLOCK) * SUBLANES, SUBLANES)
            return pltpu.make_async_copy(zero_ref, xs_ref.at[pl.ds(start, blk_rows), :], zsem)

        n_blocks = xs_ref.shape[0] // blk_rows
        first_free = gend_ref[N_EXPERTS - 1] // MOE_BLOCK

        def free_copy(j):
            start = pl.multiple_of((first_free + j) * blk_rows, blk_rows)
            return pltpu.make_async_copy(zero_ref, xs_ref.at[pl.ds(start, blk_rows), :], zsem)

        for e in range(N_EXPERTS):
            @pl.when(gpad_ref[e] > 0)
            def _():
                zero_copy(e).start()

            @pl.when(first_free + e < n_blocks)
            def _():
                free_copy(e).start()
        for e in range(N_EXPERTS):
            @pl.when(gpad_ref[e] > 0)
            def _():
                zero_copy(e).wait()

            @pl.when(first_free + e < n_blocks)
            def _():
                free_copy(e).wait()

    def start(tok, carry):
        for kk in range(TOP_K):
            _token_copy(h_ref, tok, xs_ref, dest_ref[tok * TOP_K + kk], sem).start(priority=kk % 2)
        return carry

    lax.fori_loop(0, t, start, 0, unroll=4)
    for _ in range(TOP_K):
        pltpu.make_async_copy(h_ref, xs_ref.at[pl.ds(0, t * SUBLANES), :], sem).wait()


def _moe_dispatch(h_tt, dest, pad_end, padded, n_rows):
    n = dest.shape[0]
    t = DISPATCH_TOKENS
    grid_spec = pltpu.PrefetchScalarGridSpec(
        num_scalar_prefetch=2,
        grid=(n // t,),
        in_specs=[
            pl.BlockSpec((t * TOP_K,), lambda i, ge, gp: (i,), memory_space=pltpu.SMEM),
            pl.BlockSpec((t * SUBLANES, LANES), lambda i, ge, gp: (i, 0)),
        ],
        out_specs=pl.BlockSpec(memory_space=pl.ANY),
        scratch_shapes=[
            pltpu.VMEM((MOE_BLOCK * SUBLANES, LANES), F32),
            pltpu.SemaphoreType.DMA(()),
            pltpu.SemaphoreType.DMA(()),
        ],
    )
    return pl.pallas_call(
        _dispatch_kernel,
        grid_spec=grid_spec,
        out_shape=jax.ShapeDtypeStruct((n_rows * SUBLANES, LANES), F32),
        compiler_params=_params(("arbitrary",)),
        name="moe_dispatch",
    )(pad_end, padded, dest.reshape(n * TOP_K), h_tt)


def _experts_kernel(be_ref, na_ref, xs_ref, wgu_ref, bgu_ref, wdn_ref, bdn_ref, ys_ref, wgu_bf, wdn_bf):
    i = pl.program_id(0)
    tm = MOE_BLOCK
    ff = wdn_ref.shape[2]

    @pl.when((i == 0) | (be_ref[i] != be_ref[jnp.maximum(i - 1, 0)]))
    def _():
        wgu_bf[...] = wgu_ref[0, 0].astype(BF16)
        wdn_bf[...] = wdn_ref[0, 0].astype(BF16)

    @pl.when(i < na_ref[0])
    def _():
        xb = _load_token_tiles(xs_ref, tm).astype(BF16)
        gu = _dot(xb, wgu_bf[...]) + bgu_ref[0, 0]
        gate = jnp.minimum(gu[:, :ff], SWIGLU_LIMIT)
        up = jnp.clip(gu[:, ff:], -SWIGLU_LIMIT, SWIGLU_LIMIT)
        glu = gate * _sigmoid(SWIGLU_ALPHA * gate)
        y = _dot(((up + 1.0) * glu).astype(BF16), wdn_bf[...]) + bdn_ref[0, 0]
        _store_token_tiles(ys_ref, y)

    @pl.when(i >= na_ref[0])
    def _():
        ys_ref[...] = jnp.zeros_like(ys_ref)


def _moe_experts(xs, block_e, n_active, layer, w_gu, b_gu, w_dn, b_dn):
    n_l, n_e, d, ff2 = w_gu.shape
    ff = ff2 // 2
    n_blocks = block_e.shape[0]
    tm = MOE_BLOCK
    grid_spec = pltpu.PrefetchScalarGridSpec(
        num_scalar_prefetch=2,
        grid=(n_blocks,),
        in_specs=[
            pl.BlockSpec((tm * SUBLANES, LANES), lambda i, be, na: (jnp.minimum(i, na[0] - 1), 0)),
            pl.BlockSpec((1, 1, d, ff2), lambda i, be, na: (layer, be[i], 0, 0)),
            pl.BlockSpec((1, 1, 1, ff2), lambda i, be, na: (layer, be[i], 0, 0)),
            pl.BlockSpec((1, 1, ff, d), lambda i, be, na: (layer, be[i], 0, 0)),
            pl.BlockSpec((1, 1, 1, d), lambda i, be, na: (layer, be[i], 0, 0)),
        ],
        out_specs=pl.BlockSpec((tm * SUBLANES, LANES), lambda i, be, na: (i, 0)),
        scratch_shapes=[pltpu.VMEM((d, ff2), BF16), pltpu.VMEM((ff, d), BF16)],
    )
    return pl.pallas_call(
        _experts_kernel,
        grid_spec=grid_spec,
        out_shape=jax.ShapeDtypeStruct(xs.shape, F32),
        compiler_params=_params(("arbitrary",), EXPERT_VMEM_LIMIT),
        name="moe_experts",
    )(block_e, n_active, xs, w_gu, b_gu.reshape(n_l, n_e, 1, ff2), w_dn, b_dn.reshape(n_l, n_e, 1, d))


def _combine_kernel(final_norm, dest_ref, dest_next_ref, x_ref, mod_ref, wt_ref, gfin_ref, ys_ref, o_ref,
                    buf_ref, sems):
    t = DISPATCH_TOKENS
    slot_rows = t * SUBLANES
    half_rows = TOP_K * slot_rows
    i = pl.program_id(0)
    n_tiles = pl.num_programs(0)

    def gather(idx_ref, half):
        base_tok = half * (TOP_K * t)

        def start(tok, carry):
            for kk in range(TOP_K):
                _token_copy(ys_ref, idx_ref[tok * TOP_K + kk], buf_ref, base_tok + kk * t + tok,
                            sems.at[half]).start(priority=kk % 2)
            return carry

        lax.fori_loop(0, t, start, 0, unroll=4)

    @pl.when(i == 0)
    def _():
        gather(dest_ref, 0)

    @pl.when(i + 1 < n_tiles)
    def _():
        gather(dest_next_ref, (i + 1) % 2)

    half = i % 2
    base = pl.multiple_of(half * half_rows, half_rows)
    pltpu.make_async_copy(ys_ref.at[pl.ds(0, half_rows), :], buf_ref.at[pl.ds(base, half_rows), :],
                          sems.at[half]).wait()

    wt = wt_ref[...]
    y = jnp.zeros(x_ref.shape, F32)
    for kk in range(TOP_K):
        rows = pl.ds(pl.multiple_of(base + kk * slot_rows, slot_rows), slot_rows)
        y = y + _load_token_tiles(buf_ref.at[rows, :], t) * wt[:, kk:kk + 1]
    x_new = x_ref[...] + mod_ref[0][5:6] * y
    o_ref[...] = _rms(x_new, gfin_ref[...]) if final_norm else x_new


def _moe_combine(x2, mod, wts, dest, ys, seq, g_final, final_norm):
    n, d = x2.shape
    t = DISPATCH_TOKENS
    tiles_per_b = seq // t
    n_tiles = n // t
    dest_flat = dest.reshape(n * TOP_K)
    return pl.pallas_call(
        functools.partial(_combine_kernel, final_norm),
        grid=(n_tiles,),
        in_specs=[
            pl.BlockSpec((t * TOP_K,), lambda i: (i,), memory_space=pltpu.SMEM),
            pl.BlockSpec((t * TOP_K,), lambda i: (jnp.minimum(i + 1, n_tiles - 1),), memory_space=pltpu.SMEM),
            pl.BlockSpec((t, d), lambda i: (i, 0)),
            pl.BlockSpec((1, 6, d), lambda i: (i // tiles_per_b, 0, 0)),
            pl.BlockSpec((t, TOP_K), lambda i: (i, 0)),
            pl.BlockSpec((1, d), lambda i: (0, 0)),
            pl.BlockSpec(memory_space=pl.ANY),
        ],
        out_specs=pl.BlockSpec((t, d), lambda i: (i, 0)),
        out_shape=jax.ShapeDtypeStruct((n, d), F32),
        scratch_shapes=[
            pltpu.VMEM((2 * TOP_K * t * SUBLANES, LANES), F32),
            pltpu.SemaphoreType.DMA((2,)),
        ],
        compiler_params=_params(("arbitrary",)),
        name="moe_combine",
    )(dest_flat, dest_flat, x2, mod, wts, g_final.reshape(1, d), ys)


def _moe_layer(x2, mod, g_ffn, w_router, b_router, layer, w_gu, b_gu, w_dn, b_dn, seq, g_final, final_norm):
    n = x2.shape[0]
    h_tt, idx, wts, counts = _moe_router(x2, mod, g_ffn, w_router, b_router, seq)
    counts = counts[0, :N_EXPERTS].astype(I32)
    padded = (counts + MOE_BLOCK - 1) // MOE_BLOCK * MOE_BLOCK
    pad_end = jnp.cumsum(padded)
    pad_start = pad_end - padded
    n_rows = n * TOP_K + N_EXPERTS * MOE_BLOCK
    n_blocks = n_rows // MOE_BLOCK
    blk_start = jnp.arange(n_blocks, dtype=I32) * MOE_BLOCK
    block_e = jnp.minimum(jnp.sum((pad_end[None, :] <= blk_start[:, None]).astype(I32), axis=1),
                          N_EXPERTS - 1)
    n_active = (pad_end[-1:] // MOE_BLOCK).astype(I32)
    start_vec = jnp.zeros((1, LANES), F32).at[0, :N_EXPERTS].set(pad_start.astype(F32))
    dest = _moe_dest(idx, start_vec)
    xs = _moe_dispatch(h_tt, dest, pad_end.astype(I32), padded.astype(I32), n_rows)
    ys = _moe_experts(xs, block_e, n_active, layer, w_gu, b_gu, w_dn, b_dn)
    return _moe_combine(x2, mod, wts, dest, ys, seq, g_final, final_norm)


def _rope_kernel(pos_ref, inv_ref, cos_ref, sin_ref):
    ang = pos_ref[...] * inv_ref[...]
    cos_ref[...] = jnp.cos(ang)
    sin_ref[...] = jnp.sin(ang)


def _rope_table(positions):
    half = MLA_ROPE // 2
    n = positions.size
    rep = LANES // half
    inv = np.power(np.float32(ROPE_THETA),
                   -np.arange(half, dtype=np.float32) * np.float32(2.0 / MLA_ROPE)).astype(np.float32)
    inv_row = jnp.asarray(np.tile(inv, rep).reshape(1, LANES))
    pos = jnp.repeat(positions.reshape(n).astype(F32), half).reshape(n // rep, LANES)
    rows = n // rep
    tr = min(rows, 2048)
    cos, sin = pl.pallas_call(
        _rope_kernel,
        grid=(rows // tr,),
        in_specs=[pl.BlockSpec((tr, LANES), lambda i: (i, 0)), pl.BlockSpec((1, LANES), lambda i: (0, 0))],
        out_specs=[pl.BlockSpec((tr, LANES), lambda i: (i, 0))] * 2,
        out_shape=[jax.ShapeDtypeStruct((rows, LANES), F32)] * 2,
        compiler_params=_params(("arbitrary",)),
        name="rope_table",
    )(pos, inv_row)
    return cos.reshape(n, half), sin.reshape(n, half)


def _rope_halves(t1, t2, cos, sin):
    return t1 * cos - t2 * sin, t2 * cos + t1 * sin


def _kv_kernel(n_heads, x_ref, mod_ref, g_ref, wd_ref, gc_ref, wu_ref, cos_ref, sin_ref, k_ref, v_ref):
    half = MLA_ROPE // 2
    mod = mod_ref[0]
    hk = _norm_mod(x_ref[...], g_ref[...], mod[0:1], mod[1:2]).astype(BF16)
    ckr = _dot(hk, wd_ref[...])
    c_kv = _rms(ckr[:, :MLA_KV_RANK], gc_ref[...]).astype(BF16)
    r1, r2 = _rope_halves(ckr[:, MLA_KV_RANK:MLA_KV_RANK + half], ckr[:, MLA_KV_RANK + half:],
                          cos_ref[...], sin_ref[...])
    kv = _dot(c_kv, wu_ref[...])
    nv = n_heads * MLA_NOPE
    for h in range(n_heads):
        k_ref[0, h] = jnp.concatenate(
            [kv[:, h * MLA_NOPE:(h + 1) * MLA_NOPE], r1, r2], axis=-1).astype(BF16)
        v_ref[0, h] = kv[:, nv + h * MLA_V:nv + (h + 1) * MLA_V].astype(BF16)


def _kv_shared(x2, mod_kv, g_kv, w_dkv, g_ckv, w_ukv, cos, sin, bsz, seq):
    n, d = x2.shape
    n_heads = w_ukv.shape[1] // (MLA_NOPE + MLA_V)
    half = MLA_ROPE // 2
    tm = 512
    tiles = seq // tm
    w3 = w_ukv.reshape(MLA_KV_RANK, n_heads, MLA_NOPE + MLA_V)
    wu = jnp.concatenate([w3[:, :, :MLA_NOPE].reshape(MLA_KV_RANK, -1),
                          w3[:, :, MLA_NOPE:].reshape(MLA_KV_RANK, -1)], axis=1).astype(BF16)
    row = lambda b, j: (b * tiles + j, 0)
    const = lambda b, j: (0, 0)
    qk = MLA_NOPE + MLA_ROPE
    return pl.pallas_call(
        functools.partial(_kv_kernel, n_heads),
        grid=(bsz, tiles),
        in_specs=[
            pl.BlockSpec((tm, d), row),
            pl.BlockSpec((1, 2, d), lambda b, j: (b, 0, 0)),
            pl.BlockSpec((1, d), const),
            pl.BlockSpec(w_dkv.shape, const),
            pl.BlockSpec((1, MLA_KV_RANK), const),
            pl.BlockSpec(wu.shape, const),
            pl.BlockSpec((tm, half), row),
            pl.BlockSpec((tm, half), row),
        ],
        out_specs=[
            pl.BlockSpec((1, n_heads, tm, qk), lambda b, j: (b, 0, j, 0)),
            pl.BlockSpec((1, n_heads, tm, MLA_V), lambda b, j: (b, 0, j, 0)),
        ],
        out_shape=[
            jax.ShapeDtypeStruct((bsz, n_heads, seq, qk), BF16),
            jax.ShapeDtypeStruct((bsz, n_heads, seq, MLA_V), BF16),
        ],
        compiler_params=_params(("arbitrary", "arbitrary")),
        name="kv_shared",
    )(x2, mod_kv, g_kv.reshape(1, d), w_dkv.astype(BF16), g_ckv.reshape(1, MLA_KV_RANK), wu, cos, sin)


def _mla_q_kernel(n_heads, x_ref, mod_ref, g_ref, wd_ref, gc_ref, wu_ref, cos_ref, sin_ref, q_ref):
    half = MLA_ROPE // 2
    mod = mod_ref[0]
    h = _norm_mod(x_ref[...], g_ref[...], mod[0:1], mod[1:2]).astype(BF16)
    cq = _rms(_dot(h, wd_ref[...]), gc_ref[...]).astype(BF16)
    q = _dot(cq, wu_ref[...])
    n1 = n_heads * MLA_NOPE
    n2 = n1 + n_heads * half
    cos = jnp.concatenate([cos_ref[...]] * n_heads, axis=-1)
    sin = jnp.concatenate([sin_ref[...]] * n_heads, axis=-1)
    r1, r2 = _rope_halves(q[:, n1:n2], q[:, n2:], cos, sin)
    scale = math.log2(math.e) / math.sqrt(MLA_NOPE + MLA_ROPE)
    for hh in range(n_heads):
        q_ref[0, hh] = (jnp.concatenate(
            [q[:, hh * MLA_NOPE:(hh + 1) * MLA_NOPE],
             r1[:, hh * half:(hh + 1) * half], r2[:, hh * half:(hh + 1) * half]], axis=-1) * scale).astype(BF16)


def _mla_q(x2, mod, g_mix, w_dq, g_cq, w_uq, cos, sin, bsz, seq):
    n, d = x2.shape
    q_rank = w_dq.shape[1]
    qk = MLA_NOPE + MLA_ROPE
    n_heads = w_uq.shape[1] // qk
    half = MLA_ROPE // 2
    tm = 512
    tiles = seq // tm
    w3 = w_uq.reshape(q_rank, n_heads, qk)
    wu = jnp.concatenate([w3[:, :, :MLA_NOPE].reshape(q_rank, -1),
                          w3[:, :, MLA_NOPE:MLA_NOPE + half].reshape(q_rank, -1),
                          w3[:, :, MLA_NOPE + half:].reshape(q_rank, -1)], axis=1).astype(BF16)
    row = lambda b, j: (b * tiles + j, 0)
    const = lambda b, j: (0, 0)
    return pl.pallas_call(
        functools.partial(_mla_q_kernel, n_heads),
        grid=(bsz, tiles),
        in_specs=[
            pl.BlockSpec((tm, d), row),
            pl.BlockSpec((1, 6, d), lambda b, j: (b, 0, 0)),
            pl.BlockSpec((1, d), const),
            pl.BlockSpec(w_dq.shape, const),
            pl.BlockSpec((1, q_rank), const),
            pl.BlockSpec(wu.shape, const),
            pl.BlockSpec((tm, half), row),
            pl.BlockSpec((tm, half), row),
        ],
        out_specs=pl.BlockSpec((1, n_heads, tm, qk), lambda b, j: (b, 0, j, 0)),
        out_shape=jax.ShapeDtypeStruct((bsz, n_heads, seq, qk), BF16),
        compiler_params=_params(("arbitrary", "arbitrary")),
        name="mla_q",
    )(x2, mod, g_mix.reshape(1, d), w_dq.astype(BF16), g_cq.reshape(1, q_rank), wu, cos, sin)


ATTN_BLOCK = 512
ATTN_SUB_BLOCKS = 1


def _flash_kernel(q_ref, k_ref, v_ref, o_ref):
    t = ATTN_BLOCK
    n_sub = ATTN_SUB_BLOCKS
    ts = t // n_sub
    qi = pl.program_id(2)
    dv = v_ref.shape[3]
    q_subs = [q_ref[0, 0, i * ts:(i + 1) * ts, :] for i in range(n_sub)]

    def block(carry, start, width, diagonal):
        k = k_ref[0, 0, pl.ds(start, width), :]
        v = v_ref[0, 0, pl.ds(start, width), :]
        out = []
        for i, (m_prev, l_prev, acc) in enumerate(carry):
            s = _dot_nt(q_subs[i], k)
            if diagonal:
                row = lax.broadcasted_iota(I32, (ts, width), 0) + i * ts
                col = lax.broadcasted_iota(I32, (ts, width), 1)
                s = jnp.where(col <= row, s, NEG_BIG)
            m_new = jnp.maximum(m_prev, jnp.max(s, axis=-1, keepdims=True))
            alpha = jnp.exp2(m_prev - m_new)
            pr = jnp.exp2(s - m_new)
            l_new = alpha * l_prev + jnp.sum(pr, axis=-1, keepdims=True)
            out.append((m_new, l_new, alpha * acc + _dot(pr.astype(BF16), v)))
        return tuple(out)

    init = tuple((jnp.full((ts, 1), NEG_BIG, F32), jnp.zeros((ts, 1), F32), jnp.zeros((ts, dv), F32))
                 for _ in range(n_sub))
    carry = lax.fori_loop(
        0, lax.shift_right_logical(qi, 1),
        lambda jj, c: block(c, pl.multiple_of(jj * (2 * t), 2 * t), 2 * t, False), init)
    carry = lax.cond(
        (qi & 1) == 1,
        lambda c: block(c, pl.multiple_of((qi - 1) * t, t), t, False),
        lambda c: c, carry)
    carry = block(carry, pl.multiple_of(qi * t, t), t, True)
    for i, (_, l_fin, acc) in enumerate(carry):
        o_ref[0, i * ts:(i + 1) * ts, :] = (acc / l_fin).astype(BF16)


def _flash_attn(q, k, v):
    bsz, n_heads, seq, qk = q.shape
    dv = v.shape[3]
    t = ATTN_BLOCK
    return pl.pallas_call(
        _flash_kernel,
        grid=(bsz, n_heads, seq // t),
        in_specs=[
            pl.BlockSpec((1, 1, t, qk), lambda b, h, i: (b, h, i, 0)),
            pl.BlockSpec((1, 1, seq, qk), lambda b, h, i: (b, h, 0, 0)),
            pl.BlockSpec((1, 1, seq, dv), lambda b, h, i: (b, h, 0, 0)),
        ],
        out_specs=pl.BlockSpec((1, t, dv), lambda b, h, i: (b, i, h)),
        out_shape=jax.ShapeDtypeStruct((bsz, seq, n_heads * dv), BF16),
        compiler_params=_params(("arbitrary", "arbitrary", "arbitrary")),
        name="flash_attn",
    )(q, k, v)


def _attn_out_kernel(x_ref, mod_ref, o_ref, w_ref, y_ref):
    y_ref[...] = x_ref[...] + mod_ref[0][2:3] * _dot(o_ref[...], w_ref[...])


def _attn_out(x2, mod, o2, w_o_bf, seq):
    n, d = x2.shape
    tm = 512
    tiles_per_b = seq // tm
    return pl.pallas_call(
        _attn_out_kernel,
        grid=(n // tm,),
        in_specs=[
            pl.BlockSpec((tm, d), lambda i: (i, 0)),
            pl.BlockSpec((1, 6, d), lambda i: (i // tiles_per_b, 0, 0)),
            pl.BlockSpec((tm, o2.shape[1]), lambda i: (i, 0)),
            pl.BlockSpec(w_o_bf.shape, lambda i: (0, 0)),
        ],
        out_specs=pl.BlockSpec((tm, d), lambda i: (i, 0)),
        out_shape=jax.ShapeDtypeStruct((n, d), F32),
        compiler_params=_params(("arbitrary",)),
        name="attn_out",
    )(x2, mod, o2, w_o_bf)


def _final_kernel(x_ref, g_ref, o_ref):
    o_ref[...] = _rms(x_ref[...], g_ref[...])


def _final_norm(x2, g):
    n, d = x2.shape
    tm = 1024
    return pl.pallas_call(
        _final_kernel,
        grid=(n // tm,),
        in_specs=[pl.BlockSpec((tm, d), lambda i: (i, 0)), pl.BlockSpec((1, d), lambda i: (0, 0))],
        out_specs=pl.BlockSpec((tm, d), lambda i: (i, 0)),
        out_shape=jax.ShapeDtypeStruct((n, d), F32),
        compiler_params=_params(("arbitrary",)),
        name="final_norm",
    )(x2, g.reshape(1, d))


def kernel(x, c, positions, g_mix, g_ffn, w_ada, b_ada, w_in_a, lb_logits, g_out_a, w_out_a, g_kv, w_ada_kv, b_ada_kv, w_dkv, g_ckv, w_ukv, w_dq, g_cq, w_uq, w_o_b, w_router, b_router, w_gu, b_gu, w_dn, b_dn, g_final):
    bsz, seq, d = x.shape
    depth = g_mix.shape[0]
    n_a = w_in_a.shape[0]
    n = bsz * seq
    x2 = x.reshape(n, d)

    mods = _ada_mod(c, w_ada, b_ada)
    mods = mods.reshape(depth, bsz, 6, d)
    mod_kv = _ada_mod(c, w_ada_kv[None], b_ada_kv[None]).reshape(bsz, 2, d)

    shared = None
    cos = sin = None
    for l in range(depth):
        mod = mods[l]
        if l < n_a:
            q, k, v, lf, gate = _hgrn_in(x2, mod, g_mix[l], lb_logits, w_in_a[l].astype(BF16), l, seq)
            x2 = _hgrn_rec(x2, mod, q, k, v, lf, gate, g_out_a[l], w_out_a[l].astype(BF16), bsz, seq)
        else:
            j = l - n_a
            qh = _mla_q(x2, mod, g_mix[l], w_dq[j], g_cq[j], w_uq[j], cos, sin, bsz, seq)
            o2 = _flash_attn(qh, *shared)
            x2 = _attn_out(x2, mod, o2.reshape(n, -1), w_o_b[j].astype(BF16), seq)
        fuse_final = l == depth - 1 and l != n_a - 1
        x2 = _moe_layer(x2, mod, g_ffn[l], w_router[l], b_router[l],
                        l, w_gu, b_gu, w_dn, b_dn, seq, g_final, fuse_final)
        if l == n_a - 1:
            cos, sin = _rope_table(positions)
            shared = _kv_shared(x2, mod_kv, g_kv, w_dkv, g_ckv, w_ukv, cos, sin, bsz, seq)
    if not fuse_final:
        x2 = _final_norm(x2, g_final)
    return x2.reshape(bsz, seq, d)
```

```python
import functools
import math

import numpy as np
import jax
import jax.numpy as jnp
from jax import lax
from jax.experimental import pallas as pl
from jax.experimental.pallas import tpu as pltpu

F32 = jnp.float32
BF16 = jnp.bfloat16
I32 = jnp.int32

LANES = 128
SUBLANES = 8
VMEM_LIMIT = 48 * 1024 * 1024
EXPERT_VMEM_LIMIT = 56 * 1024 * 1024

HG_HEAD_DIM = 128
HG_CHUNK = 256
FORGET_FLOOR = 1e-30
MLA_NOPE = 128
MLA_ROPE = 64
MLA_V = 128
MLA_KV_RANK = 256
ROPE_THETA = 10000.0
N_EXPERTS = 32
TOP_K = 4
SWIGLU_LIMIT = 7.0
SWIGLU_ALPHA = 1.702
MOE_BLOCK = 512
NORM_EPS = 1e-6
NEG_BIG = -1e30

N_LEVELS = int(math.log2(HG_CHUNK))


def _params(sem, vmem=VMEM_LIMIT):
    return pltpu.CompilerParams(dimension_semantics=sem, vmem_limit_bytes=vmem)


def _dot(a, b):
    return jnp.dot(a, b, preferred_element_type=F32)


def _dot_nt(a, b):
    return lax.dot_general(a, b, (((1,), (1,)), ((), ())), preferred_element_type=F32)


def _split2(a):
    hi = a.astype(BF16)
    lo = (a - hi.astype(F32)).astype(BF16)
    return hi, lo


def _dot3(a, b):
    a_hi, a_lo = _split2(a)
    b_hi, b_lo = _split2(b)
    return _dot(a_hi, b_hi) + _dot(a_lo, b_hi) + _dot(a_hi, b_lo)


def _sigmoid(x):
    return 1.0 / (1.0 + jnp.exp(-x))


def _rms(x, gain):
    ms = jnp.mean(x * x, axis=-1, keepdims=True)
    return x * lax.rsqrt(ms + NORM_EPS) * gain


def _norm_mod(x, gain, shift, scale):
    return _rms(x, gain) * (1.0 + scale) + shift


def _ada_kernel(c_ref, w_ref, b_ref, o_ref):
    c = c_ref[...]
    c_act = c * _sigmoid(c)
    o_ref[0] = _dot3(c_act, w_ref[0]) + b_ref[0]


def _ada_mod(c, w, b):
    n_l, d, m = w.shape
    bsz = c.shape[0]
    tn = 1024
    return pl.pallas_call(
        _ada_kernel,
        grid=(n_l, m // tn),
        in_specs=[
            pl.BlockSpec((bsz, d), lambda l, j: (0, 0)),
            pl.BlockSpec((1, d, tn), lambda l, j: (l, 0, j)),
            pl.BlockSpec((1, 1, tn), lambda l, j: (l, 0, j)),
        ],
        out_specs=pl.BlockSpec((1, bsz, tn), lambda l, j: (l, 0, j)),
        out_shape=jax.ShapeDtypeStruct((n_l, bsz, m), F32),
        compiler_params=_params(("arbitrary", "arbitrary")),
        name="ada_mod",
    )(c, w, b.reshape(n_l, 1, m))


def _hgrn_in_kernel(layer, n_a, x_ref, mod_ref, g_ref, lbl_ref, w_ref,
                    q_ref, k_ref, v_ref, lf_ref, gate_ref):
    d = x_ref.shape[1]
    mod = mod_ref[0]
    h = _norm_mod(x_ref[...], g_ref[...], mod[0:1], mod[1:2]).astype(BF16)
    rows = [lbl_ref[i:i + 1, :] for i in range(n_a)]
    mx = functools.reduce(jnp.maximum, rows)
    ex = [jnp.exp(r - mx) for r in rows]
    den = functools.reduce(lambda a, b: a + b, ex)
    lb = (functools.reduce(lambda a, b: a + b, ex[:layer + 1]) - ex[0]) / den

    yq = _dot(h, w_ref[:, 0:d])
    q_ref[...] = (yq * _sigmoid(yq)).astype(BF16)
    yf = _dot(h, w_ref[:, d:2 * d])
    sg = _sigmoid(yf)
    forget = lb + (1.0 - lb) * sg
    lf_ref[...] = jnp.log(jnp.maximum(forget, FORGET_FLOOR))
    k_ref[...] = ((1.0 - lb) * (1.0 - sg)).astype(BF16)
    v_ref[...] = _dot(h, w_ref[:, 2 * d:3 * d]).astype(BF16)
    yg = _dot(h, w_ref[:, 3 * d:4 * d])
    gate_ref[...] = (yg * _sigmoid(yg)).astype(BF16)


def _hgrn_in(x2, mod, g_mix, lb_logits, w_in_bf, layer, seq):
    n, d = x2.shape
    tm = 512
    n_a = lb_logits.shape[0]
    tiles_per_b = seq // tm
    row = lambda i: (i, 0)
    return pl.pallas_call(
        functools.partial(_hgrn_in_kernel, layer, n_a),
        grid=(n // tm,),
        in_specs=[
            pl.BlockSpec((tm, d), row),
            pl.BlockSpec((1, 6, d), lambda i: (i // tiles_per_b, 0, 0)),
            pl.BlockSpec((1, d), lambda i: (0, 0)),
            pl.BlockSpec((n_a, d), lambda i: (0, 0)),
            pl.BlockSpec((d, 4 * d), lambda i: (0, 0)),
        ],
        out_specs=[pl.BlockSpec((tm, d), row)] * 5,
        out_shape=[jax.ShapeDtypeStruct((n, d), BF16)] * 3
        + [jax.ShapeDtypeStruct((n, d), F32), jax.ShapeDtypeStruct((n, d), BF16)],
        compiler_params=_params(("arbitrary",)),
        name="hgrn_in",
    )(x2, mod, g_mix.reshape(1, d), lb_logits, w_in_bf)


def _mid_rows(cum, half):
    c, d = cum.shape
    blk = 2 * half
    if blk >= SUBLANES:
        parts = [jnp.broadcast_to(cum[b * blk + half - 1:b * blk + half, :], (blk, d))
                 for b in range(c // blk)]
        return jnp.concatenate(parts, axis=0)
    sub = lax.broadcasted_iota(I32, (SUBLANES, 1), 0)
    parts = []
    for g in range(c // SUBLANES):
        out = None
        for j in range(SUBLANES // blk):
            r = g * SUBLANES + j * blk + half - 1
            cand = jnp.broadcast_to(cum[r:r + 1, :], (SUBLANES, d))
            out = cand if out is None else jnp.where(sub >= j * blk, cand, out)
        parts.append(out)
    return jnp.concatenate(parts, axis=0)


def _hgrn_rec_kernel(n_heads, x_ref, mod_ref, q_ref, k_ref, v_ref, lf_ref, gate_ref,
                     tri_ref, gout_ref, wout_ref, o_ref, state_ref, obuf_ref):
    c = HG_CHUNK
    hd = HG_HEAD_DIM
    ts = x_ref.shape[0]

    @pl.when(pl.program_id(1) == 0)
    def _():
        state_ref[...] = jnp.zeros_like(state_ref)

    t_idx = lax.broadcasted_iota(I32, (c, c), 0)
    s_idx = lax.broadcasted_iota(I32, (c, c), 1)
    r_idx = lax.broadcasted_iota(I32, (c, 1), 0)
    pair_masks = []
    right_rows = []
    for lvl in range(N_LEVELS):
        pair_masks.append(jnp.logical_and(t_idx > s_idx, ((t_idx ^ s_idx) >> lvl) == 1))
        right_rows.append(((r_idx >> lvl) & 1) == 1)
    diag_mask = t_idx == s_idx
    tri = tri_ref[...]
    gout = gout_ref[...]

    def chunk_body(ci, carry):
        r0 = pl.multiple_of(ci * c, c)
        rows = pl.ds(r0, c)
        lf = lf_ref[rows, :]
        qa = q_ref[rows, :].astype(F32)
        ka = k_ref[rows, :].astype(F32)
        va = v_ref[rows, :].astype(F32)
        l_hi = lf.astype(BF16)
        rem = lf - l_hi.astype(F32)
        l_mid = rem.astype(BF16)
        l_lo = (rem - l_mid.astype(F32)).astype(BF16)
        cum = _dot(tri, l_hi) + _dot(tri, l_mid) + _dot(tri, l_lo)
        e_cum = jnp.exp(cum)
        e_tail = jnp.exp(jnp.broadcast_to(cum[c - 1:c, :], cum.shape) - cum)
        q_in = (qa * e_cum).astype(BF16)
        k_out = (ka * e_tail).astype(BF16)
        vb = va.astype(BF16)
        qk_diag = qa * ka
        qk_sub = qa * jnp.exp(lf) * pltpu.roll(ka, 1, 0)
        xs = []
        for lvl in range(1, N_LEVELS):
            half = 1 << lvl
            if half >= SUBLANES:
                r_parts, x_parts = [], []
                for lo in range(0, c, 2 * half):
                    mid, hi = lo + half, lo + 2 * half
                    m_row = jnp.broadcast_to(cum[mid - 1:mid, :], (half, cum.shape[1]))
                    r_parts += [m_row - cum[lo:mid, :], cum[mid:hi, :] - m_row]
                    x_parts += [ka[lo:mid, :], qa[mid:hi, :]]
                e = jnp.exp(jnp.concatenate(r_parts, axis=0))
                xs.append((jnp.concatenate(x_parts, axis=0) * e).astype(BF16))
            else:
                e = jnp.exp(-jnp.abs(cum - _mid_rows(cum, half)))
                xs.append((jnp.where(right_rows[lvl], qa, ka) * e).astype(BF16))
        for h in range(n_heads):
            cols = slice(h * hd, (h + 1) * hd)
            d_col = jnp.sum(qk_diag[:, cols], axis=-1, keepdims=True)
            s_col = jnp.sum(qk_sub[:, cols], axis=-1, keepdims=True)
            sc = jnp.where(diag_mask, d_col, jnp.where(pair_masks[0], s_col, 0.0))
            for lvl in range(1, N_LEVELS):
                xl = xs[lvl - 1][:, cols]
                sc = jnp.where(pair_masks[lvl], _dot_nt(xl, xl), sc)
            st = state_ref[h]
            o = _dot_nt(q_in[:, cols], st.astype(BF16)) + _dot(sc.astype(BF16), vb[:, cols])
            upd = _dot(va[:, cols].T.astype(BF16), k_out[:, cols])
            state_ref[h] = st * e_cum[c - 1:c, cols] + upd
            on = _rms(o, gout) * gate_ref[rows, cols].astype(F32)
            obuf_ref[rows, cols] = on.astype(BF16)
        return carry

    lax.fori_loop(0, ts // c, chunk_body, 0)
    y = _dot(obuf_ref[...], wout_ref[...])
    o_ref[...] = x_ref[...] + mod_ref[0][2:3] * y


def _hgrn_rec(x2, mod, q, k, v, lf, gate, g_out, w_out_bf, bsz, seq):
    n, d = x2.shape
    ts = 512
    n_heads = d // HG_HEAD_DIM
    tiles = seq // ts
    row = lambda b, j: (b * tiles + j, 0)
    const = lambda b, j: (0, 0)
    tri = jnp.asarray(np.tril(np.ones((HG_CHUNK, HG_CHUNK), np.float32)), BF16)
    return pl.pallas_call(
        functools.partial(_hgrn_rec_kernel, n_heads),
        grid=(bsz, tiles),
        in_specs=[
            pl.BlockSpec((ts, d), row),
            pl.BlockSpec((1, 6, d), lambda b, j: (b, 0, 0)),
            pl.BlockSpec((ts, d), row), pl.BlockSpec((ts, d), row), pl.BlockSpec((ts, d), row),
            pl.BlockSpec((ts, d), row), pl.BlockSpec((ts, d), row),
            pl.BlockSpec(tri.shape, const),
            pl.BlockSpec((1, HG_HEAD_DIM), const),
            pl.BlockSpec((d, d), const),
        ],
        out_specs=pl.BlockSpec((ts, d), row),
        out_shape=jax.ShapeDtypeStruct((n, d), F32),
        scratch_shapes=[
            pltpu.VMEM((n_heads, HG_HEAD_DIM, HG_HEAD_DIM), F32),
            pltpu.VMEM((ts, d), BF16),
        ],
        compiler_params=_params(("arbitrary", "arbitrary")),
        name="hgrn_rec",
    )(x2, mod, q, k, v, lf, gate, tri, g_out.reshape(1, HG_HEAD_DIM), w_out_bf)


def _store_token_tiles(ref, val):
    tm = val.shape[0]
    for cidx in range(val.shape[1] // LANES):
        ref[pl.ds(cidx, tm, stride=SUBLANES), :] = val[:, cidx * LANES:(cidx + 1) * LANES]


def _load_token_tiles(ref, tm):
    n_c = ref.shape[0] // tm
    return jnp.concatenate([ref[pl.ds(cidx, tm, stride=SUBLANES), :] for cidx in range(n_c)], axis=-1)


def _router_kernel(x_ref, mod_ref, g_ref, wr_ref, br_ref, h_ref, idx_ref, wt_ref, cnt_ref):
    tm = x_ref.shape[0]
    mod = mod_ref[0]
    h = _norm_mod(x_ref[...], g_ref[...], mod[3:4], mod[4:5])
    _store_token_tiles(h_ref, h)
    logits = _dot3(h, wr_ref[...]) + br_ref[...]
    lane = lax.broadcasted_iota(I32, (tm, LANES), 1)
    lane_f = lane.astype(F32)
    idx_acc = jnp.zeros((tm, LANES), F32)
    val_acc = jnp.zeros((tm, LANES), F32)
    cnt = jnp.zeros((tm, LANES), F32)
    work = logits
    for kk in range(TOP_K):
        m = jnp.max(work, axis=-1, keepdims=True)
        sel = jnp.min(jnp.where(work == m, lane_f, float(LANES)), axis=-1, keepdims=True)
        hit = lane_f == sel
        work = jnp.where(hit, -jnp.inf, work)
        cnt = cnt + jnp.where(hit, 1.0, 0.0)
        idx_acc = jnp.where(lane == kk, sel, idx_acc)
        val_acc = jnp.where(lane == kk, m, val_acc)
    top = jnp.max(jnp.where(lane < TOP_K, val_acc, -jnp.inf), axis=-1, keepdims=True)
    ex = jnp.where(lane < TOP_K, jnp.exp(val_acc - top), 0.0)
    wts = ex / jnp.sum(ex, axis=-1, keepdims=True)
    idx_ref[...] = idx_acc[:, :TOP_K].astype(I32)
    wt_ref[...] = wts[:, :TOP_K]

    @pl.when(pl.program_id(0) == 0)
    def _():
        cnt_ref[...] = jnp.zeros_like(cnt_ref)

    cnt_ref[...] += jnp.sum(cnt, axis=0, keepdims=True)


def _moe_router(x2, mod, g_ffn, w_router, b_router, seq):
    n, d = x2.shape
    tm = 512
    tiles_per_b = seq // tm
    wr = jnp.zeros((d, LANES), F32).at[:, :N_EXPERTS].set(w_router)
    br = jnp.full((1, LANES), NEG_BIG, F32).at[0, :N_EXPERTS].set(b_router)
    return pl.pallas_call(
        _router_kernel,
        grid=(n // tm,),
        in_specs=[
            pl.BlockSpec((tm, d), lambda i: (i, 0)),
            pl.BlockSpec((1, 6, d), lambda i: (i // tiles_per_b, 0, 0)),
            pl.BlockSpec((1, d), lambda i: (0, 0)),
            pl.BlockSpec((d, LANES), lambda i: (0, 0)),
            pl.BlockSpec((1, LANES), lambda i: (0, 0)),
        ],
        out_specs=[
            pl.BlockSpec((tm * SUBLANES, LANES), lambda i: (i, 0)),
            pl.BlockSpec((tm, TOP_K), lambda i: (i, 0)),
            pl.BlockSpec((tm, TOP_K), lambda i: (i, 0)),
            pl.BlockSpec((1, LANES), lambda i: (0, 0)),
        ],
        out_shape=[
            jax.ShapeDtypeStruct((n * SUBLANES, LANES), F32),
            jax.ShapeDtypeStruct((n, TOP_K), I32),
            jax.ShapeDtypeStruct((n, TOP_K), F32),
            jax.ShapeDtypeStruct((1, LANES), F32),
        ],
        compiler_params=_params(("arbitrary",)),
        name="moe_router",
    )(x2, mod, g_ffn.reshape(1, d), wr, br)


def _dest_kernel(idx_ref, start_ref, tri_ref, dest_ref, carry_ref):
    tm = idx_ref.shape[0]

    @pl.when(pl.program_id(0) == 0)
    def _():
        carry_ref[...] = jnp.zeros_like(carry_ref)

    lane = lax.broadcasted_iota(I32, (tm, LANES), 1)
    idx = idx_ref[...]
    hits = [lane == idx[:, kk:kk + 1] for kk in range(TOP_K)]
    cnt = functools.reduce(lambda a, b: a + b, [jnp.where(hh, 1.0, 0.0) for hh in hits])
    before = _dot(tri_ref[...], cnt.astype(BF16))
    pos = before + carry_ref[...] + start_ref[...]
    acc = jnp.zeros((tm, LANES), F32)
    for kk in range(TOP_K):
        dk = jnp.sum(jnp.where(hits[kk], pos, 0.0), axis=-1, keepdims=True)
        acc = jnp.where(lane == kk, dk, acc)
    dest_ref[...] = acc[:, :TOP_K].astype(I32)
    carry_ref[...] += jnp.sum(cnt, axis=0, keepdims=True)


def _moe_dest(idx, pad_start):
    n = idx.shape[0]
    tm = 1024
    tri = jnp.asarray(np.tril(np.ones((tm, tm), np.float32), -1), BF16)
    return pl.pallas_call(
        _dest_kernel,
        grid=(n // tm,),
        in_specs=[
            pl.BlockSpec((tm, TOP_K), lambda i: (i, 0)),
            pl.BlockSpec((1, LANES), lambda i: (0, 0)),
            pl.BlockSpec((tm, tm), lambda i: (0, 0)),
        ],
        out_specs=pl.BlockSpec((tm, TOP_K), lambda i: (i, 0)),
        out_shape=jax.ShapeDtypeStruct((n, TOP_K), I32),
        scratch_shapes=[pltpu.VMEM((1, LANES), F32)],
        compiler_params=_params(("arbitrary",)),
        name="moe_dest",
    )(idx, pad_start, tri)


DISPATCH_TOKENS = 512


def _token_copy(src, src_tok, dst, dst_tok, sem):
    s0 = pl.multiple_of(src_tok * SUBLANES, SUBLANES)
    d0 = pl.multiple_of(dst_tok * SUBLANES, SUBLANES)
    return pltpu.make_async_copy(src.at[pl.ds(s0, SUBLANES), :], dst.at[pl.ds(d0, SUBLANES), :], sem)


def _dispatch_kernel(gend_ref, gpad_ref, dest_ref, h_ref, xs_ref, zero_ref, sem, zsem):
    t = DISPATCH_TOKENS
    blk_rows = MOE_BLOCK * SUBLANES

    @pl.when(pl.program_id(0) == 0)
    def _():
        zero_ref[...] = jnp.zeros_like(zero_ref)

        def zero_copy(e):
            start = pl.multiple_of((gend_ref[e] - MOE_BLOCK) * SUBLANES, SUBLANES)
            return pltpu.make_async_copy(zero_ref, xs_ref.at[pl.ds(start, blk_rows), :], zsem)

        n_blocks = xs_ref.shape[0] // blk_rows
        first_free = gend_ref[N_EXPERTS - 1] // MOE_BLOCK

        def free_copy(j):
            start = pl.multiple_of((first_free + j) * blk_rows, blk_rows)
            return pltpu.make_async_copy(zero_ref, xs_ref.at[pl.ds(start, blk_rows), :], zsem)

        for e in range(N_EXPERTS):
            @pl.when(gpad_ref[e] > 0)
            def _():
                zero_copy(e).start()

            @pl.when(first_free + e < n_blocks)
            def _():
                free_copy(e).start()
        for e in range(N_EXPERTS):
            @pl.when(gpad_ref[e] > 0)
            def _():
                zero_copy(e).wait()

            @pl.when(first_free + e < n_blocks)
            def _():
                free_copy(e).wait()

    def start(tok, carry):
        for kk in range(TOP_K):
            _token_copy(h_ref, tok, xs_ref, dest_ref[tok * TOP_K + kk], sem).start(priority=kk % 2)
        return carry

    lax.fori_loop(0, t, start, 0, unroll=4)
    for _ in range(TOP_K):
        pltpu.make_async_copy(h_ref, xs_ref.at[pl.ds(0, t * SUBLANES), :], sem).wait()


def _moe_dispatch(h_tt, dest, pad_end, padded, n_rows):
    n = dest.shape[0]
    t = DISPATCH_TOKENS
    grid_spec = pltpu.PrefetchScalarGridSpec(
        num_scalar_prefetch=2,
        grid=(n // t,),
        in_specs=[
            pl.BlockSpec((t * TOP_K,), lambda i, ge, gp: (i,), memory_space=pltpu.SMEM),
            pl.BlockSpec((t * SUBLANES, LANES), lambda i, ge, gp: (i, 0)),
        ],
        out_specs=pl.BlockSpec(memory_space=pl.ANY),
        scratch_shapes=[
            pltpu.VMEM((MOE_BLOCK * SUBLANES, LANES), F32),
            pltpu.SemaphoreType.DMA(()),
            pltpu.SemaphoreType.DMA(()),
        ],
    )
    return pl.pallas_call(
        _dispatch_kernel,
        grid_spec=grid_spec,
        out_shape=jax.ShapeDtypeStruct((n_rows * SUBLANES, LANES), F32),
        compiler_params=_params(("arbitrary",)),
        name="moe_dispatch",
    )(pad_end, padded, dest.reshape(n * TOP_K), h_tt)


def _experts_kernel(be_ref, na_ref, xs_ref, wgu_ref, bgu_ref, wdn_ref, bdn_ref, ys_ref, wgu_bf, wdn_bf):
    i = pl.program_id(0)
    tm = MOE_BLOCK
    ff = wdn_ref.shape[2]

    @pl.when((i == 0) | (be_ref[i] != be_ref[jnp.maximum(i - 1, 0)]))
    def _():
        wgu_bf[...] = wgu_ref[0, 0].astype(BF16)
        wdn_bf[...] = wdn_ref[0, 0].astype(BF16)

    @pl.when(i < na_ref[0])
    def _():
        xb = _load_token_tiles(xs_ref, tm).astype(BF16)
        gu = _dot(xb, wgu_bf[...]) + bgu_ref[0, 0]
        gate = jnp.minimum(gu[:, :ff], SWIGLU_LIMIT)
        up = jnp.clip(gu[:, ff:], -SWIGLU_LIMIT, SWIGLU_LIMIT)
        glu = gate * _sigmoid(SWIGLU_ALPHA * gate)
        y = _dot(((up + 1.0) * glu).astype(BF16), wdn_bf[...]) + bdn_ref[0, 0]
        _store_token_tiles(ys_ref, y)

    @pl.when(i >= na_ref[0])
    def _():
        ys_ref[...] = jnp.zeros_like(ys_ref)


def _moe_experts(xs, block_e, n_active, layer, w_gu, b_gu, w_dn, b_dn):
    n_l, n_e, d, ff2 = w_gu.shape
    ff = ff2 // 2
    n_blocks = block_e.shape[0]
    tm = MOE_BLOCK
    grid_spec = pltpu.PrefetchScalarGridSpec(
        num_scalar_prefetch=2,
        grid=(n_blocks,),
        in_specs=[
            pl.BlockSpec((tm * SUBLANES, LANES), lambda i, be, na: (jnp.minimum(i, na[0] - 1), 0)),
            pl.BlockSpec((1, 1, d, ff2), lambda i, be, na: (layer, be[i], 0, 0)),
            pl.BlockSpec((1, 1, 1, ff2), lambda i, be, na: (layer, be[i], 0, 0)),
            pl.BlockSpec((1, 1, ff, d), lambda i, be, na: (layer, be[i], 0, 0)),
            pl.BlockSpec((1, 1, 1, d), lambda i, be, na: (layer, be[i], 0, 0)),
        ],
        out_specs=pl.BlockSpec((tm * SUBLANES, LANES), lambda i, be, na: (i, 0)),
        scratch_shapes=[pltpu.VMEM((d, ff2), BF16), pltpu.VMEM((ff, d), BF16)],
    )
    return pl.pallas_call(
        _experts_kernel,
        grid_spec=grid_spec,
        out_shape=jax.ShapeDtypeStruct(xs.shape, F32),
        compiler_params=_params(("arbitrary",), EXPERT_VMEM_LIMIT),
        name="moe_experts",
    )(block_e, n_active, xs, w_gu, b_gu.reshape(n_l, n_e, 1, ff2), w_dn, b_dn.reshape(n_l, n_e, 1, d))


def _combine_kernel(final_norm, dest_ref, dest_next_ref, x_ref, mod_ref, wt_ref, gfin_ref, ys_ref, o_ref,
                    buf_ref, sems):
    t = DISPATCH_TOKENS
    slot_rows = t * SUBLANES
    half_rows = TOP_K * slot_rows
    i = pl.program_id(0)
    n_tiles = pl.num_programs(0)

    def gather(idx_ref, half):
        base_tok = half * (TOP_K * t)

        def start(tok, carry):
            for kk in range(TOP_K):
                _token_copy(ys_ref, idx_ref[tok * TOP_K + kk], buf_ref, base_tok + kk * t + tok,
                            sems.at[half]).start(priority=kk % 2)
            return carry

        lax.fori_loop(0, t, start, 0, unroll=4)

    @pl.when(i == 0)
    def _():
        gather(dest_ref, 0)

    @pl.when(i + 1 < n_tiles)
    def _():
        gather(dest_next_ref, (i + 1) % 2)

    half = i % 2
    base = pl.multiple_of(half * half_rows, half_rows)
    pltpu.make_async_copy(ys_ref.at[pl.ds(0, half_rows), :], buf_ref.at[pl.ds(base, half_rows), :],
                          sems.at[half]).wait()

    wt = wt_ref[...]
    y = jnp.zeros(x_ref.shape, F32)
    for kk in range(TOP_K):
        rows = pl.ds(pl.multiple_of(base + kk * slot_rows, slot_rows), slot_rows)
        y = y + _load_token_tiles(buf_ref.at[rows, :], t) * wt[:, kk:kk + 1]
    x_new = x_ref[...] + mod_ref[0][5:6] * y
    o_ref[...] = _rms(x_new, gfin_ref[...]) if final_norm else x_new


def _moe_combine(x2, mod, wts, dest, ys, seq, g_final, final_norm):
    n, d = x2.shape
    t = DISPATCH_TOKENS
    tiles_per_b = seq // t
    n_tiles = n // t
    dest_flat = dest.reshape(n * TOP_K)
    return pl.pallas_call(
        functools.partial(_combine_kernel, final_norm),
        grid=(n_tiles,),
        in_specs=[
            pl.BlockSpec((t * TOP_K,), lambda i: (i,), memory_space=pltpu.SMEM),
            pl.BlockSpec((t * TOP_K,), lambda i: (jnp.minimum(i + 1, n_tiles - 1),), memory_space=pltpu.SMEM),
            pl.BlockSpec((t, d), lambda i: (i, 0)),
            pl.BlockSpec((1, 6, d), lambda i: (i // tiles_per_b, 0, 0)),
            pl.BlockSpec((t, TOP_K), lambda i: (i, 0)),
            pl.BlockSpec((1, d), lambda i: (0, 0)),
            pl.BlockSpec(memory_space=pl.ANY),
        ],
        out_specs=pl.BlockSpec((t, d), lambda i: (i, 0)),
        out_shape=jax.ShapeDtypeStruct((n, d), F32),
        scratch_shapes=[
            pltpu.VMEM((2 * TOP_K * t * SUBLANES, LANES), F32),
            pltpu.SemaphoreType.DMA((2,)),
        ],
        compiler_params=_params(("arbitrary",)),
        name="moe_combine",
    )(dest_flat, dest_flat, x2, mod, wts, g_final.reshape(1, d), ys)


def _moe_layer(x2, mod, g_ffn, w_router, b_router, layer, w_gu, b_gu, w_dn, b_dn, seq, g_final, final_norm):
    n = x2.shape[0]
    h_tt, idx, wts, counts = _moe_router(x2, mod, g_ffn, w_router, b_router, seq)
    counts = counts[0, :N_EXPERTS].astype(I32)
    padded = (counts + MOE_BLOCK - 1) // MOE_BLOCK * MOE_BLOCK
    pad_end = jnp.cumsum(padded)
    pad_start = pad_end - padded
    n_rows = n * TOP_K + N_EXPERTS * MOE_BLOCK
    n_blocks = n_rows // MOE_BLOCK
    blk_start = jnp.arange(n_blocks, dtype=I32) * MOE_BLOCK
    block_e = jnp.minimum(jnp.sum((pad_end[None, :] <= blk_start[:, None]).astype(I32), axis=1),
                          N_EXPERTS - 1)
    n_active = (pad_end[-1:] // MOE_BLOCK).astype(I32)
    start_vec = jnp.zeros((1, LANES), F32).at[0, :N_EXPERTS].set(pad_start.astype(F32))
    dest = _moe_dest(idx, start_vec)
    xs = _moe_dispatch(h_tt, dest, pad_end.astype(I32), padded.astype(I32), n_rows)
    ys = _moe_experts(xs, block_e, n_active, layer, w_gu, b_gu, w_dn, b_dn)
    return _moe_combine(x2, mod, wts, dest, ys, seq, g_final, final_norm)


def _rope_kernel(pos_ref, inv_ref, cos_ref, sin_ref):
    ang = pos_ref[...] * inv_ref[...]
    cos_ref[...] = jnp.cos(ang)
    sin_ref[...] = jnp.sin(ang)


def _rope_table(positions):
    half = MLA_ROPE // 2
    n = positions.size
    rep = LANES // half
    inv = np.power(np.float32(ROPE_THETA),
                   -np.arange(half, dtype=np.float32) * np.float32(2.0 / MLA_ROPE)).astype(np.float32)
    inv_row = jnp.asarray(np.tile(inv, rep).reshape(1, LANES))
    pos = jnp.repeat(positions.reshape(n).astype(F32), half).reshape(n // rep, LANES)
    rows = n // rep
    tr = min(rows, 2048)
    cos, sin = pl.pallas_call(
        _rope_kernel,
        grid=(rows // tr,),
        in_specs=[pl.BlockSpec((tr, LANES), lambda i: (i, 0)), pl.BlockSpec((1, LANES), lambda i: (0, 0))],
        out_specs=[pl.BlockSpec((tr, LANES), lambda i: (i, 0))] * 2,
        out_shape=[jax.ShapeDtypeStruct((rows, LANES), F32)] * 2,
        compiler_params=_params(("arbitrary",)),
        name="rope_table",
    )(pos, inv_row)
    return cos.reshape(n, half), sin.reshape(n, half)


def _rope_halves(t1, t2, cos, sin):
    return t1 * cos - t2 * sin, t2 * cos + t1 * sin


def _kv_kernel(n_heads, x_ref, mod_ref, g_ref, wd_ref, gc_ref, wu_ref, cos_ref, sin_ref, k_ref, v_ref):
    half = MLA_ROPE // 2
    mod = mod_ref[0]
    hk = _norm_mod(x_ref[...], g_ref[...], mod[0:1], mod[1:2]).astype(BF16)
    ckr = _dot(hk, wd_ref[...])
    c_kv = _rms(ckr[:, :MLA_KV_RANK], gc_ref[...]).astype(BF16)
    r1, r2 = _rope_halves(ckr[:, MLA_KV_RANK:MLA_KV_RANK + half], ckr[:, MLA_KV_RANK + half:],
                          cos_ref[...], sin_ref[...])
    kv = _dot(c_kv, wu_ref[...])
    nv = n_heads * MLA_NOPE
    for h in range(n_heads):
        k_ref[0, h] = jnp.concatenate(
            [kv[:, h * MLA_NOPE:(h + 1) * MLA_NOPE], r1, r2], axis=-1).astype(BF16)
        v_ref[0, h] = kv[:, nv + h * MLA_V:nv + (h + 1) * MLA_V].astype(BF16)


def _kv_shared(x2, mod_kv, g_kv, w_dkv, g_ckv, w_ukv, cos, sin, bsz, seq):
    n, d = x2.shape
    n_heads = w_ukv.shape[1] // (MLA_NOPE + MLA_V)
    half = MLA_ROPE // 2
    tm = 512
    tiles = seq // tm
    w3 = w_ukv.reshape(MLA_KV_RANK, n_heads, MLA_NOPE + MLA_V)
    wu = jnp.concatenate([w3[:, :, :MLA_NOPE].reshape(MLA_KV_RANK, -1),
                          w3[:, :, MLA_NOPE:].reshape(MLA_KV_RANK, -1)], axis=1).astype(BF16)
    row = lambda b, j: (b * tiles + j, 0)
    const = lambda b, j: (0, 0)
    qk = MLA_NOPE + MLA_ROPE
    return pl.pallas_call(
        functools.partial(_kv_kernel, n_heads),
        grid=(bsz, tiles),
        in_specs=[
            pl.BlockSpec((tm, d), row),
            pl.BlockSpec((1, 2, d), lambda b, j: (b, 0, 0)),
            pl.BlockSpec((1, d), const),
            pl.BlockSpec(w_dkv.shape, const),
            pl.BlockSpec((1, MLA_KV_RANK), const),
            pl.BlockSpec(wu.shape, const),
            pl.BlockSpec((tm, half), row),
            pl.BlockSpec((tm, half), row),
        ],
        out_specs=[
            pl.BlockSpec((1, n_heads, tm, qk), lambda b, j: (b, 0, j, 0)),
            pl.BlockSpec((1, n_heads, tm, MLA_V), lambda b, j: (b, 0, j, 0)),
        ],
        out_shape=[
            jax.ShapeDtypeStruct((bsz, n_heads, seq, qk), BF16),
            jax.ShapeDtypeStruct((bsz, n_heads, seq, MLA_V), BF16),
        ],
        compiler_params=_params(("arbitrary", "arbitrary")),
        name="kv_shared",
    )(x2, mod_kv, g_kv.reshape(1, d), w_dkv.astype(BF16), g_ckv.reshape(1, MLA_KV_RANK), wu, cos, sin)


def _mla_q_kernel(n_heads, x_ref, mod_ref, g_ref, wd_ref, gc_ref, wu_ref, cos_ref, sin_ref, q_ref):
    half = MLA_ROPE // 2
    mod = mod_ref[0]
    h = _norm_mod(x_ref[...], g_ref[...], mod[0:1], mod[1:2]).astype(BF16)
    cq = _rms(_dot(h, wd_ref[...]), gc_ref[...]).astype(BF16)
    q = _dot(cq, wu_ref[...])
    n1 = n_heads * MLA_NOPE
    n2 = n1 + n_heads * half
    cos = jnp.concatenate([cos_ref[...]] * n_heads, axis=-1)
    sin = jnp.concatenate([sin_ref[...]] * n_heads, axis=-1)
    r1, r2 = _rope_halves(q[:, n1:n2], q[:, n2:], cos, sin)
    scale = math.log2(math.e) / math.sqrt(MLA_NOPE + MLA_ROPE)
    for hh in range(n_heads):
        q_ref[0, hh] = (jnp.concatenate(
            [q[:, hh * MLA_NOPE:(hh + 1) * MLA_NOPE],
             r1[:, hh * half:(hh + 1) * half], r2[:, hh * half:(hh + 1) * half]], axis=-1) * scale).astype(BF16)


def _mla_q(x2, mod, g_mix, w_dq, g_cq, w_uq, cos, sin, bsz, seq):
    n, d = x2.shape
    q_rank = w_dq.shape[1]
    qk = MLA_NOPE + MLA_ROPE
    n_heads = w_uq.shape[1] // qk
    half = MLA_ROPE // 2
    tm = 512
    tiles = seq // tm
    w3 = w_uq.reshape(q_rank, n_heads, qk)
    wu = jnp.concatenate([w3[:, :, :MLA_NOPE].reshape(q_rank, -1),
                          w3[:, :, MLA_NOPE:MLA_NOPE + half].reshape(q_rank, -1),
                          w3[:, :, MLA_NOPE + half:].reshape(q_rank, -1)], axis=1).astype(BF16)
    row = lambda b, j: (b * tiles + j, 0)
    const = lambda b, j: (0, 0)
    return pl.pallas_call(
        functools.partial(_mla_q_kernel, n_heads),
        grid=(bsz, tiles),
        in_specs=[
            pl.BlockSpec((tm, d), row),
            pl.BlockSpec((1, 6, d), lambda b, j: (b, 0, 0)),
            pl.BlockSpec((1, d), const),
            pl.BlockSpec(w_dq.shape, const),
            pl.BlockSpec((1, q_rank), const),
            pl.BlockSpec(wu.shape, const),
            pl.BlockSpec((tm, half), row),
            pl.BlockSpec((tm, half), row),
        ],
        out_specs=pl.BlockSpec((1, n_heads, tm, qk), lambda b, j: (b, 0, j, 0)),
        out_shape=jax.ShapeDtypeStruct((bsz, n_heads, seq, qk), BF16),
        compiler_params=_params(("arbitrary", "arbitrary")),
        name="mla_q",
    )(x2, mod, g_mix.reshape(1, d), w_dq.astype(BF16), g_cq.reshape(1, q_rank), wu, cos, sin)


ATTN_BLOCK = 512
ATTN_SUB_BLOCKS = 1


def _flash_kernel(q_ref, k_ref, v_ref, o_ref):
    t = ATTN_BLOCK
    n_sub = ATTN_SUB_BLOCKS
    ts = t // n_sub
    qi = pl.program_id(2)
    dv = v_ref.shape[3]
    q_subs = [q_ref[0, 0, i * ts:(i + 1) * ts, :] for i in range(n_sub)]

    def block(carry, start, width, diagonal):
        k = k_ref[0, 0, pl.ds(start, width), :]
        v = v_ref[0, 0, pl.ds(start, width), :]
        out = []
        for i, (m_prev, l_prev, acc) in enumerate(carry):
            s = _dot_nt(q_subs[i], k)
            if diagonal:
                row = lax.broadcasted_iota(I32, (ts, width), 0) + i * ts
                col = lax.broadcasted_iota(I32, (ts, width), 1)
                s = jnp.where(col <= row, s, NEG_BIG)
            m_new = jnp.maximum(m_prev, jnp.max(s, axis=-1, keepdims=True))
            alpha = jnp.exp2(m_prev - m_new)
            pr = jnp.exp2(s - m_new)
            l_new = alpha * l_prev + jnp.sum(pr, axis=-1, keepdims=True)
            out.append((m_new, l_new, alpha * acc + _dot(pr.astype(BF16), v)))
        return tuple(out)

    init = tuple((jnp.full((ts, 1), NEG_BIG, F32), jnp.zeros((ts, 1), F32), jnp.zeros((ts, dv), F32))
                 for _ in range(n_sub))
    n_quads = lax.shift_right_logical(qi, 2)
    carry = lax.fori_loop(
        0, n_quads,
        lambda jj, c: block(c, pl.multiple_of(jj * (4 * t), 4 * t), 4 * t, False), init)
    carry = lax.cond(
        (qi & 2) == 2,
        lambda c: block(c, pl.multiple_of(n_quads * (4 * t), 2 * t), 2 * t, False),
        lambda c: c, carry)
    carry = lax.cond(
        (qi & 1) == 1,
        lambda c: block(c, pl.multiple_of((qi - 1) * t, t), t, False),
        lambda c: c, carry)
    carry = block(carry, pl.multiple_of(qi * t, t), t, True)
    for i, (_, l_fin, acc) in enumerate(carry):
        o_ref[0, i * ts:(i + 1) * ts, :] = (acc / l_fin).astype(BF16)


def _flash_attn(q, k, v):
    bsz, n_heads, seq, qk = q.shape
    dv = v.shape[3]
    t = ATTN_BLOCK
    return pl.pallas_call(
        _flash_kernel,
        grid=(bsz, n_heads, seq // t),
        in_specs=[
            pl.BlockSpec((1, 1, t, qk), lambda b, h, i: (b, h, i, 0)),
            pl.BlockSpec((1, 1, seq, qk), lambda b, h, i: (b, h, 0, 0)),
            pl.BlockSpec((1, 1, seq, dv), lambda b, h, i: (b, h, 0, 0)),
        ],
        out_specs=pl.BlockSpec((1, t, dv), lambda b, h, i: (b, i, h)),
        out_shape=jax.ShapeDtypeStruct((bsz, seq, n_heads * dv), BF16),
        compiler_params=_params(("arbitrary", "arbitrary", "arbitrary")),
        name="flash_attn",
    )(q, k, v)


def _attn_out_kernel(x_ref, mod_ref, o_ref, w_ref, y_ref):
    y_ref[...] = x_ref[...] + mod_ref[0][2:3] * _dot(o_ref[...], w_ref[...])


def _attn_out(x2, mod, o2, w_o_bf, seq):
    n, d = x2.shape
    tm = 512
    tiles_per_b = seq // tm
    return pl.pallas_call(
        _attn_out_kernel,
        grid=(n // tm,),
        in_specs=[
            pl.BlockSpec((tm, d), lambda i: (i, 0)),
            pl.BlockSpec((1, 6, d), lambda i: (i // tiles_per_b, 0, 0)),
            pl.BlockSpec((tm, o2.shape[1]), lambda i: (i, 0)),
            pl.BlockSpec(w_o_bf.shape, lambda i: (0, 0)),
        ],
        out_specs=pl.BlockSpec((tm, d), lambda i: (i, 0)),
        out_shape=jax.ShapeDtypeStruct((n, d), F32),
        compiler_params=_params(("arbitrary",)),
        name="attn_out",
    )(x2, mod, o2, w_o_bf)


def _final_kernel(x_ref, g_ref, o_ref):
    o_ref[...] = _rms(x_ref[...], g_ref[...])


def _final_norm(x2, g):
    n, d = x2.shape
    tm = 1024
    return pl.pallas_call(
        _final_kernel,
        grid=(n // tm,),
        in_specs=[pl.BlockSpec((tm, d), lambda i: (i, 0)), pl.BlockSpec((1, d), lambda i: (0, 0))],
        out_specs=pl.BlockSpec((tm, d), lambda i: (i, 0)),
        out_shape=jax.ShapeDtypeStruct((n, d), F32),
        compiler_params=_params(("arbitrary",)),
        name="final_norm",
    )(x2, g.reshape(1, d))


def kernel(x, c, positions, g_mix, g_ffn, w_ada, b_ada, w_in_a, lb_logits, g_out_a, w_out_a, g_kv, w_ada_kv, b_ada_kv, w_dkv, g_ckv, w_ukv, w_dq, g_cq, w_uq, w_o_b, w_router, b_router, w_gu, b_gu, w_dn, b_dn, g_final):
    bsz, seq, d = x.shape
    depth = g_mix.shape[0]
    n_a = w_in_a.shape[0]
    n = bsz * seq
    x2 = x.reshape(n, d)

    mods = _ada_mod(c, w_ada, b_ada)
    mods = mods.reshape(depth, bsz, 6, d)
    mod_kv = _ada_mod(c, w_ada_kv[None], b_ada_kv[None]).reshape(bsz, 2, d)

    shared = None
    cos = sin = None
    for l in range(depth):
        mod = mods[l]
        if l < n_a:
            q, k, v, lf, gate = _hgrn_in(x2, mod, g_mix[l], lb_logits, w_in_a[l].astype(BF16), l, seq)
            x2 = _hgrn_rec(x2, mod, q, k, v, lf, gate, g_out_a[l], w_out_a[l].astype(BF16), bsz, seq)
        else:
            j = l - n_a
            qh = _mla_q(x2, mod, g_mix[l], w_dq[j], g_cq[j], w_uq[j], cos, sin, bsz, seq)
            o2 = _flash_attn(qh, *shared)
            x2 = _attn_out(x2, mod, o2.reshape(n, -1), w_o_b[j].astype(BF16), seq)
        fuse_final = l == depth - 1 and l != n_a - 1
        x2 = _moe_layer(x2, mod, g_ffn[l], w_router[l], b_router[l],
                        l, w_gu, b_gu, w_dn, b_dn, seq, g_final, fuse_final)
        if l == n_a - 1:
            cos, sin = _rope_table(positions)
            shared = _kv_shared(x2, mod_kv, g_kv, w_dkv, g_ckv, w_ukv, cos, sin, bsz, seq)
    if not fuse_final:
        x2 = _final_norm(x2, g_final)
    return x2.reshape(bsz, seq, d)
```

```python
import functools
import math

import numpy as np
import jax
import jax.numpy as jnp
from jax import lax
from jax.experimental import pallas as pl
from jax.experimental.pallas import tpu as pltpu

F32 = jnp.float32
BF16 = jnp.bfloat16
I32 = jnp.int32

LANES = 128
SUBLANES = 8
VMEM_LIMIT = 48 * 1024 * 1024
EXPERT_VMEM_LIMIT = 56 * 1024 * 1024

HG_HEAD_DIM = 128
HG_CHUNK = 256
FORGET_FLOOR = 1e-30
MLA_NOPE = 128
MLA_ROPE = 64
MLA_V = 128
MLA_KV_RANK = 256
ROPE_THETA = 10000.0
N_EXPERTS = 32
TOP_K = 4
SWIGLU_LIMIT = 7.0
SWIGLU_ALPHA = 1.702
MOE_BLOCK = 512
NORM_EPS = 1e-6
NEG_BIG = -1e30

N_LEVELS = int(math.log2(HG_CHUNK))


def _params(sem, vmem=VMEM_LIMIT):
    return pltpu.CompilerParams(dimension_semantics=sem, vmem_limit_bytes=vmem)


def _dot(a, b):
    return jnp.dot(a, b, preferred_element_type=F32)


def _dot_nt(a, b):
    return lax.dot_general(a, b, (((1,), (1,)), ((), ())), preferred_element_type=F32)


def _split2(a):
    hi = a.astype(BF16)
    lo = (a - hi.astype(F32)).astype(BF16)
    return hi, lo


def _dot3(a, b):
    a_hi, a_lo = _split2(a)
    b_hi, b_lo = _split2(b)
    return _dot(a_hi, b_hi) + _dot(a_lo, b_hi) + _dot(a_hi, b_lo)


def _sigmoid(x):
    return 1.0 / (1.0 + jnp.exp(-x))


def _rms(x, gain):
    ms = jnp.mean(x * x, axis=-1, keepdims=True)
    return x * lax.rsqrt(ms + NORM_EPS) * gain


def _norm_mod(x, gain, shift, scale):
    return _rms(x, gain) * (1.0 + scale) + shift


def _ada_kernel(c_ref, w_ref, b_ref, o_ref):
    c = c_ref[...]
    c_act = c * _sigmoid(c)
    o_ref[0] = _dot3(c_act, w_ref[0]) + b_ref[0]


def _ada_mod(c, w, b):
    n_l, d, m = w.shape
    bsz = c.shape[0]
    tn = 1024
    return pl.pallas_call(
        _ada_kernel,
        grid=(n_l, m // tn),
        in_specs=[
            pl.BlockSpec((bsz, d), lambda l, j: (0, 0)),
            pl.BlockSpec((1, d, tn), lambda l, j: (l, 0, j)),
            pl.BlockSpec((1, 1, tn), lambda l, j: (l, 0, j)),
        ],
        out_specs=pl.BlockSpec((1, bsz, tn), lambda l, j: (l, 0, j)),
        out_shape=jax.ShapeDtypeStruct((n_l, bsz, m), F32),
        compiler_params=_params(("arbitrary", "arbitrary")),
        name="ada_mod",
    )(c, w, b.reshape(n_l, 1, m))


def _hgrn_in_kernel(layer, n_a, x_ref, mod_ref, g_ref, lbl_ref, w_ref,
                    q_ref, k_ref, v_ref, lf_ref, gate_ref):
    d = x_ref.shape[1]
    mod = mod_ref[0]
    h = _norm_mod(x_ref[...], g_ref[...], mod[0:1], mod[1:2]).astype(BF16)
    rows = [lbl_ref[i:i + 1, :] for i in range(n_a)]
    mx = functools.reduce(jnp.maximum, rows)
    ex = [jnp.exp(r - mx) for r in rows]
    den = functools.reduce(lambda a, b: a + b, ex)
    lb = (functools.reduce(lambda a, b: a + b, ex[:layer + 1]) - ex[0]) / den

    yq = _dot(h, w_ref[:, 0:d])
    q_ref[...] = (yq * _sigmoid(yq)).astype(BF16)
    yf = _dot(h, w_ref[:, d:2 * d])
    sg = _sigmoid(yf)
    forget = lb + (1.0 - lb) * sg
    lf_ref[...] = jnp.log(jnp.maximum(forget, FORGET_FLOOR))
    k_ref[...] = ((1.0 - lb) * (1.0 - sg)).astype(BF16)
    v_ref[...] = _dot(h, w_ref[:, 2 * d:3 * d]).astype(BF16)
    yg = _dot(h, w_ref[:, 3 * d:4 * d])
    gate_ref[...] = (yg * _sigmoid(yg)).astype(BF16)


def _hgrn_in(x2, mod, g_mix, lb_logits, w_in_bf, layer, seq):
    n, d = x2.shape
    tm = 512
    n_a = lb_logits.shape[0]
    tiles_per_b = seq // tm
    row = lambda i: (i, 0)
    return pl.pallas_call(
        functools.partial(_hgrn_in_kernel, layer, n_a),
        grid=(n // tm,),
        in_specs=[
            pl.BlockSpec((tm, d), row),
            pl.BlockSpec((1, 6, d), lambda i: (i // tiles_per_b, 0, 0)),
            pl.BlockSpec((1, d), lambda i: (0, 0)),
            pl.BlockSpec((n_a, d), lambda i: (0, 0)),
            pl.BlockSpec((d, 4 * d), lambda i: (0, 0)),
        ],
        out_specs=[pl.BlockSpec((tm, d), row)] * 5,
        out_shape=[jax.ShapeDtypeStruct((n, d), BF16)] * 3
        + [jax.ShapeDtypeStruct((n, d), F32), jax.ShapeDtypeStruct((n, d), BF16)],
        compiler_params=_params(("arbitrary",)),
        name="hgrn_in",
    )(x2, mod, g_mix.reshape(1, d), lb_logits, w_in_bf)


def _mid_rows(cum, half):
    c, d = cum.shape
    blk = 2 * half
    if blk >= SUBLANES:
        parts = [jnp.broadcast_to(cum[b * blk + half - 1:b * blk + half, :], (blk, d))
                 for b in range(c // blk)]
        return jnp.concatenate(parts, axis=0)
    sub = lax.broadcasted_iota(I32, (SUBLANES, 1), 0)
    parts = []
    for g in range(c // SUBLANES):
        out = None
        for j in range(SUBLANES // blk):
            r = g * SUBLANES + j * blk + half - 1
            cand = jnp.broadcast_to(cum[r:r + 1, :], (SUBLANES, d))
            out = cand if out is None else jnp.where(sub >= j * blk, cand, out)
        parts.append(out)
    return jnp.concatenate(parts, axis=0)


def _hgrn_rec_kernel(n_heads, x_ref, mod_ref, q_ref, k_ref, v_ref, lf_ref, gate_ref,
                     tri_ref, gout_ref, wout_ref, o_ref, state_ref, obuf_ref):
    c = HG_CHUNK
    hd = HG_HEAD_DIM
    ts = x_ref.shape[0]

    @pl.when(pl.program_id(1) == 0)
    def _():
        state_ref[...] = jnp.zeros_like(state_ref)

    hc = c // 2
    t_idx = lax.broadcasted_iota(I32, (hc, hc), 0)
    s_idx = lax.broadcasted_iota(I32, (hc, hc), 1)
    r_idx = lax.broadcasted_iota(I32, (c, 1), 0)
    pair_masks = []
    right_rows = []
    for lvl in range(N_LEVELS):
        pair_masks.append(jnp.logical_and(t_idx > s_idx, ((t_idx ^ s_idx) >> lvl) == 1))
        right_rows.append(((r_idx >> lvl) & 1) == 1)
    diag_mask = t_idx == s_idx
    tri = tri_ref[...]
    gout = gout_ref[...]

    def chunk_body(ci, carry):
        r0 = pl.multiple_of(ci * c, c)
        rows = pl.ds(r0, c)
        lf = lf_ref[rows, :]
        qa = q_ref[rows, :].astype(F32)
        ka = k_ref[rows, :].astype(F32)
        va = v_ref[rows, :].astype(F32)
        l_hi = lf.astype(BF16)
        rem = lf - l_hi.astype(F32)
        l_mid = rem.astype(BF16)
        l_lo = (rem - l_mid.astype(F32)).astype(BF16)
        cum = _dot(tri, l_hi) + _dot(tri, l_mid) + _dot(tri, l_lo)
        e_cum = jnp.exp(cum)
        e_tail = jnp.exp(jnp.broadcast_to(cum[c - 1:c, :], cum.shape) - cum)
        q_in = (qa * e_cum).astype(BF16)
        k_out = (ka * e_tail).astype(BF16)
        vb = va.astype(BF16)
        qk_diag = qa * ka
        qk_sub = qa * jnp.exp(lf) * pltpu.roll(ka, 1, 0)
        xs = []
        for lvl in range(1, N_LEVELS):
            half = 1 << lvl
            if half >= SUBLANES:
                r_parts, x_parts = [], []
                for lo in range(0, c, 2 * half):
                    mid, hi = lo + half, lo + 2 * half
                    m_row = jnp.broadcast_to(cum[mid - 1:mid, :], (half, cum.shape[1]))
                    r_parts += [m_row - cum[lo:mid, :], cum[mid:hi, :] - m_row]
                    x_parts += [ka[lo:mid, :], qa[mid:hi, :]]
                e = jnp.exp(jnp.concatenate(r_parts, axis=0))
                xs.append((jnp.concatenate(x_parts, axis=0) * e).astype(BF16))
            else:
                e = jnp.exp(-jnp.abs(cum - _mid_rows(cum, half)))
                xs.append((jnp.where(right_rows[lvl], qa, ka) * e).astype(BF16))
        for h in range(n_heads):
            cols = slice(h * hd, (h + 1) * hd)
            d_col = jnp.sum(qk_diag[:, cols], axis=-1, keepdims=True)
            s_col = jnp.sum(qk_sub[:, cols], axis=-1, keepdims=True)
            halves = []
            for b in range(2):
                rh = slice(b * hc, (b + 1) * hc)
                sc_h = jnp.where(diag_mask, d_col[rh], jnp.where(pair_masks[0], s_col[rh], 0.0))
                for lvl in range(1, N_LEVELS - 1):
                    xl = xs[lvl - 1][rh, cols]
                    sc_h = jnp.where(pair_masks[lvl], _dot_nt(xl, xl), sc_h)
                halves.append(sc_h)
            x_top = xs[N_LEVELS - 2][:, cols]
            top = _dot_nt(x_top[hc:], x_top[:hc])
            sc = jnp.concatenate(
                [jnp.concatenate([halves[0], jnp.zeros((hc, hc), F32)], axis=1),
                 jnp.concatenate([top, halves[1]], axis=1)], axis=0)
            st = state_ref[h]
            o = _dot_nt(q_in[:, cols], st.astype(BF16)) + _dot(sc.astype(BF16), vb[:, cols])
            upd = _dot(va[:, cols].T.astype(BF16), k_out[:, cols])
            state_ref[h] = st * e_cum[c - 1:c, cols] + upd
            on = _rms(o, gout) * gate_ref[rows, cols].astype(F32)
            obuf_ref[rows, cols] = on.astype(BF16)
        return carry

    lax.fori_loop(0, ts // c, chunk_body, 0)
    y = _dot(obuf_ref[...], wout_ref[...])
    o_ref[...] = x_ref[...] + mod_ref[0][2:3] * y


def _hgrn_rec(x2, mod, q, k, v, lf, gate, g_out, w_out_bf, bsz, seq):
    n, d = x2.shape
    ts = 512
    n_heads = d // HG_HEAD_DIM
    tiles = seq // ts
    row = lambda b, j: (b * tiles + j, 0)
    const = lambda b, j: (0, 0)
    tri = jnp.asarray(np.tril(np.ones((HG_CHUNK, HG_CHUNK), np.float32)), BF16)
    return pl.pallas_call(
        functools.partial(_hgrn_rec_kernel, n_heads),
        grid=(bsz, tiles),
        in_specs=[
            pl.BlockSpec((ts, d), row),
            pl.BlockSpec((1, 6, d), lambda b, j: (b, 0, 0)),
            pl.BlockSpec((ts, d), row), pl.BlockSpec((ts, d), row), pl.BlockSpec((ts, d), row),
            pl.BlockSpec((ts, d), row), pl.BlockSpec((ts, d), row),
            pl.BlockSpec(tri.shape, const),
            pl.BlockSpec((1, HG_HEAD_DIM), const),
            pl.BlockSpec((d, d), const),
        ],
        out_specs=pl.BlockSpec((ts, d), row),
        out_shape=jax.ShapeDtypeStruct((n, d), F32),
        scratch_shapes=[
            pltpu.VMEM((n_heads, HG_HEAD_DIM, HG_HEAD_DIM), F32),
            pltpu.VMEM((ts, d), BF16),
        ],
        compiler_params=_params(("arbitrary", "arbitrary")),
        name="hgrn_rec",
    )(x2, mod, q, k, v, lf, gate, tri, g_out.reshape(1, HG_HEAD_DIM), w_out_bf)


def _store_token_tiles(ref, val):
    tm = val.shape[0]
    for cidx in range(val.shape[1] // LANES):
        ref[pl.ds(cidx, tm, stride=SUBLANES), :] = val[:, cidx * LANES:(cidx + 1) * LANES]


def _load_token_tiles(ref, tm):
    n_c = ref.shape[0] // tm
    return jnp.concatenate([ref[pl.ds(cidx, tm, stride=SUBLANES), :] for cidx in range(n_c)], axis=-1)


def _router_kernel(x_ref, mod_ref, g_ref, wr_ref, br_ref, h_ref, idx_ref, wt_ref, cnt_ref):
    tm = x_ref.shape[0]
    mod = mod_ref[0]
    h = _norm_mod(x_ref[...], g_ref[...], mod[3:4], mod[4:5])
    _store_token_tiles(h_ref, h)
    logits = _dot3(h, wr_ref[...]) + br_ref[...]
    lane = lax.broadcasted_iota(I32, (tm, LANES), 1)
    lane_f = lane.astype(F32)
    idx_acc = jnp.zeros((tm, LANES), F32)
    val_acc = jnp.zeros((tm, LANES), F32)
    cnt = jnp.zeros((tm, LANES), F32)
    work = logits
    for kk in range(TOP_K):
        m = jnp.max(work, axis=-1, keepdims=True)
        sel = jnp.min(jnp.where(work == m, lane_f, float(LANES)), axis=-1, keepdims=True)
        hit = lane_f == sel
        work = jnp.where(hit, -jnp.inf, work)
        cnt = cnt + jnp.where(hit, 1.0, 0.0)
        idx_acc = jnp.where(lane == kk, sel, idx_acc)
        val_acc = jnp.where(lane == kk, m, val_acc)
    top = jnp.max(jnp.where(lane < TOP_K, val_acc, -jnp.inf), axis=-1, keepdims=True)
    ex = jnp.where(lane < TOP_K, jnp.exp(val_acc - top), 0.0)
    wts = ex / jnp.sum(ex, axis=-1, keepdims=True)
    idx_ref[...] = idx_acc[:, :TOP_K].astype(I32)
    wt_ref[...] = wts[:, :TOP_K]

    @pl.when(pl.program_id(0) == 0)
    def _():
        cnt_ref[...] = jnp.zeros_like(cnt_ref)

    cnt_ref[...] += jnp.sum(cnt, axis=0, keepdims=True)


def _moe_router(x2, mod, g_ffn, w_router, b_router, seq):
    n, d = x2.shape
    tm = 512
    tiles_per_b = seq // tm
    wr = jnp.zeros((d, LANES), F32).at[:, :N_EXPERTS].set(w_router)
    br = jnp.full((1, LANES), NEG_BIG, F32).at[0, :N_EXPERTS].set(b_router)
    return pl.pallas_call(
        _router_kernel,
        grid=(n // tm,),
        in_specs=[
            pl.BlockSpec((tm, d), lambda i: (i, 0)),
            pl.BlockSpec((1, 6, d), lambda i: (i // tiles_per_b, 0, 0)),
            pl.BlockSpec((1, d), lambda i: (0, 0)),
            pl.BlockSpec((d, LANES), lambda i: (0, 0)),
            pl.BlockSpec((1, LANES), lambda i: (0, 0)),
        ],
        out_specs=[
            pl.BlockSpec((tm * SUBLANES, LANES), lambda i: (i, 0)),
            pl.BlockSpec((tm, TOP_K), lambda i: (i, 0)),
            pl.BlockSpec((tm, TOP_K), lambda i: (i, 0)),
            pl.BlockSpec((1, LANES), lambda i: (0, 0)),
        ],
        out_shape=[
            jax.ShapeDtypeStruct((n * SUBLANES, LANES), F32),
            jax.ShapeDtypeStruct((n, TOP_K), I32),
            jax.ShapeDtypeStruct((n, TOP_K), F32),
            jax.ShapeDtypeStruct((1, LANES), F32),
        ],
        compiler_params=_params(("arbitrary",)),
        name="moe_router",
    )(x2, mod, g_ffn.reshape(1, d), wr, br)


def _dest_kernel(idx_ref, start_ref, tri_ref, dest_ref, carry_ref):
    tm = idx_ref.shape[0]

    @pl.when(pl.program_id(0) == 0)
    def _():
        carry_ref[...] = jnp.zeros_like(carry_ref)

    lane = lax.broadcasted_iota(I32, (tm, LANES), 1)
    idx = idx_ref[...]
    hits = [lane == idx[:, kk:kk + 1] for kk in range(TOP_K)]
    cnt = functools.reduce(lambda a, b: a + b, [jnp.where(hh, 1.0, 0.0) for hh in hits])
    before = _dot(tri_ref[...], cnt.astype(BF16))
    pos = before + carry_ref[...] + start_ref[...]
    acc = jnp.zeros((tm, LANES), F32)
    for kk in range(TOP_K):
        dk = jnp.sum(jnp.where(hits[kk], pos, 0.0), axis=-1, keepdims=True)
        acc = jnp.where(lane == kk, dk, acc)
    dest_ref[...] = acc[:, :TOP_K].astype(I32)
    carry_ref[...] += jnp.sum(cnt, axis=0, keepdims=True)


def _moe_dest(idx, pad_start):
    n = idx.shape[0]
    tm = 1024
    tri = jnp.asarray(np.tril(np.ones((tm, tm), np.float32), -1), BF16)
    return pl.pallas_call(
        _dest_kernel,
        grid=(n // tm,),
        in_specs=[
            pl.BlockSpec((tm, TOP_K), lambda i: (i, 0)),
            pl.BlockSpec((1, LANES), lambda i: (0, 0)),
            pl.BlockSpec((tm, tm), lambda i: (0, 0)),
        ],
        out_specs=pl.BlockSpec((tm, TOP_K), lambda i: (i, 0)),
        out_shape=jax.ShapeDtypeStruct((n, TOP_K), I32),
        scratch_shapes=[pltpu.VMEM((1, LANES), F32)],
        compiler_params=_params(("arbitrary",)),
        name="moe_dest",
    )(idx, pad_start, tri)


DISPATCH_TOKENS = 512
COMBINE_TOKENS = 256


def _token_copy(src, src_tok, dst, dst_tok, sem):
    s0 = pl.multiple_of(src_tok * SUBLANES, SUBLANES)
    d0 = pl.multiple_of(dst_tok * SUBLANES, SUBLANES)
    return pltpu.make_async_copy(src.at[pl.ds(s0, SUBLANES), :], dst.at[pl.ds(d0, SUBLANES), :], sem)


def _dispatch_kernel(gend_ref, gpad_ref, dest_ref, h_ref, xs_ref, zero_ref, sem, zsem):
    t = DISPATCH_TOKENS
    blk_rows = MOE_BLOCK * SUBLANES

    @pl.when(pl.program_id(0) == 0)
    def _():
        zero_ref[...] = jnp.zeros_like(zero_ref)

        def zero_copy(e):
            start = pl.multiple_of((gend_ref[e] - MOE_BLOCK) * SUBLANES, SUBLANES)
            return pltpu.make_async_copy(zero_ref, xs_ref.at[pl.ds(start, blk_rows), :], zsem)

        n_blocks = xs_ref.shape[0] // blk_rows
        first_free = gend_ref[N_EXPERTS - 1] // MOE_BLOCK

        def free_copy(j):
            start = pl.multiple_of((first_free + j) * blk_rows, blk_rows)
            return pltpu.make_async_copy(zero_ref, xs_ref.at[pl.ds(start, blk_rows), :], zsem)

        for e in range(N_EXPERTS):
            @pl.when(gpad_ref[e] > 0)
            def _():
                zero_copy(e).start()

            @pl.when(first_free + e < n_blocks)
            def _():
                free_copy(e).start()
        for e in range(N_EXPERTS):
            @pl.when(gpad_ref[e] > 0)
            def _():
                zero_copy(e).wait()

            @pl.when(first_free + e < n_blocks)
            def _():
                free_copy(e).wait()

    def start(tok, carry):
        for kk in range(TOP_K):
            _token_copy(h_ref, tok, xs_ref, dest_ref[tok * TOP_K + kk], sem).start(priority=kk % 2)
        return carry

    lax.fori_loop(0, t, start, 0, unroll=4)
    for _ in range(TOP_K):
        pltpu.make_async_copy(h_ref, xs_ref.at[pl.ds(0, t * SUBLANES), :], sem).wait()


def _moe_dispatch(h_tt, dest, pad_end, padded, n_rows):
    n = dest.shape[0]
    t = DISPATCH_TOKENS
    grid_spec = pltpu.PrefetchScalarGridSpec(
        num_scalar_prefetch=2,
        grid=(n // t,),
        in_specs=[
            pl.BlockSpec((t * TOP_K,), lambda i, ge, gp: (i,), memory_space=pltpu.SMEM),
            pl.BlockSpec((t * SUBLANES, LANES), lambda i, ge, gp: (i, 0)),
        ],
        out_specs=pl.BlockSpec(memory_space=pl.ANY),
        scratch_shapes=[
            pltpu.VMEM((MOE_BLOCK * SUBLANES, LANES), F32),
            pltpu.SemaphoreType.DMA(()),
            pltpu.SemaphoreType.DMA(()),
        ],
    )
    return pl.pallas_call(
        _dispatch_kernel,
        grid_spec=grid_spec,
        out_shape=jax.ShapeDtypeStruct((n_rows * SUBLANES, LANES), F32),
        compiler_params=_params(("arbitrary",)),
        name="moe_dispatch",
    )(pad_end, padded, dest.reshape(n * TOP_K), h_tt)


def _experts_kernel(be_ref, na_ref, xs_ref, wgu_ref, bgu_ref, wdn_ref, bdn_ref, ys_ref, wgu_bf, wdn_bf):
    i = pl.program_id(0)
    tm = MOE_BLOCK
    ff = wdn_ref.shape[2]

    @pl.when((i == 0) | (be_ref[i] != be_ref[jnp.maximum(i - 1, 0)]))
    def _():
        wgu_bf[...] = wgu_ref[0, 0].astype(BF16)
        wdn_bf[...] = wdn_ref[0, 0].astype(BF16)

    @pl.when(i < na_ref[0])
    def _():
        xb = _load_token_tiles(xs_ref, tm).astype(BF16)
        gu = _dot(xb, wgu_bf[...]) + bgu_ref[0, 0]
        gate = jnp.minimum(gu[:, :ff], SWIGLU_LIMIT)
        up = jnp.clip(gu[:, ff:], -SWIGLU_LIMIT, SWIGLU_LIMIT)
        glu = gate * _sigmoid(SWIGLU_ALPHA * gate)
        y = _dot(((up + 1.0) * glu).astype(BF16), wdn_bf[...]) + bdn_ref[0, 0]
        _store_token_tiles(ys_ref, y)

    @pl.when(i >= na_ref[0])
    def _():
        ys_ref[...] = jnp.zeros_like(ys_ref)


def _moe_experts(xs, block_e, n_active, layer, w_gu, b_gu, w_dn, b_dn):
    n_l, n_e, d, ff2 = w_gu.shape
    ff = ff2 // 2
    n_blocks = block_e.shape[0]
    tm = MOE_BLOCK
    grid_spec = pltpu.PrefetchScalarGridSpec(
        num_scalar_prefetch=2,
        grid=(n_blocks,),
        in_specs=[
            pl.BlockSpec((tm * SUBLANES, LANES), lambda i, be, na: (jnp.minimum(i, na[0] - 1), 0)),
            pl.BlockSpec((1, 1, d, ff2), lambda i, be, na: (layer, be[i], 0, 0)),
            pl.BlockSpec((1, 1, 1, ff2), lambda i, be, na: (layer, be[i], 0, 0)),
            pl.BlockSpec((1, 1, ff, d), lambda i, be, na: (layer, be[i], 0, 0)),
            pl.BlockSpec((1, 1, 1, d), lambda i, be, na: (layer, be[i], 0, 0)),
        ],
        out_specs=pl.BlockSpec((tm * SUBLANES, LANES), lambda i, be, na: (i, 0)),
        scratch_shapes=[pltpu.VMEM((d, ff2), BF16), pltpu.VMEM((ff, d), BF16)],
    )
    return pl.pallas_call(
        _experts_kernel,
        grid_spec=grid_spec,
        out_shape=jax.ShapeDtypeStruct(xs.shape, F32),
        compiler_params=_params(("arbitrary",), EXPERT_VMEM_LIMIT),
        name="moe_experts",
    )(block_e, n_active, xs, w_gu, b_gu.reshape(n_l, n_e, 1, ff2), w_dn, b_dn.reshape(n_l, n_e, 1, d))


def _combine_kernel(final_norm, dest_ref, dest_next_ref, x_ref, mod_ref, wt_ref, gfin_ref, ys_ref, o_ref,
                    buf_ref, sems):
    t = COMBINE_TOKENS
    slot_rows = t * SUBLANES
    half_rows = TOP_K * slot_rows
    i = pl.program_id(0)
    n_tiles = pl.num_programs(0)

    def gather(idx_ref, half):
        base_tok = half * (TOP_K * t)

        def start(tok, carry):
            for kk in range(TOP_K):
                _token_copy(ys_ref, idx_ref[tok * TOP_K + kk], buf_ref, base_tok + kk * t + tok,
                            sems.at[half]).start(priority=kk % 2)
            return carry

        lax.fori_loop(0, t, start, 0, unroll=4)

    @pl.when(i == 0)
    def _():
        gather(dest_ref, 0)

    @pl.when(i + 1 < n_tiles)
    def _():
        gather(dest_next_ref, (i + 1) % 2)

    half = i % 2
    base = pl.multiple_of(half * half_rows, half_rows)
    pltpu.make_async_copy(ys_ref.at[pl.ds(0, half_rows), :], buf_ref.at[pl.ds(base, half_rows), :],
                          sems.at[half]).wait()

    wt = wt_ref[...]
    y = jnp.zeros(x_ref.shape, F32)
    for kk in range(TOP_K):
        rows = pl.ds(pl.multiple_of(base + kk * slot_rows, slot_rows), slot_rows)
        y = y + _load_token_tiles(buf_ref.at[rows, :], t) * wt[:, kk:kk + 1]
    x_new = x_ref[...] + mod_ref[0][5:6] * y
    o_ref[...] = _rms(x_new, gfin_ref[...]) if final_norm else x_new


def _moe_combine(x2, mod, wts, dest, ys, seq, g_final, final_norm):
    n, d = x2.shape
    t = COMBINE_TOKENS
    tiles_per_b = seq // t
    n_tiles = n // t
    dest_flat = dest.reshape(n * TOP_K)
    return pl.pallas_call(
        functools.partial(_combine_kernel, final_norm),
        grid=(n_tiles,),
        in_specs=[
            pl.BlockSpec((t * TOP_K,), lambda i: (i,), memory_space=pltpu.SMEM),
            pl.BlockSpec((t * TOP_K,), lambda i: (jnp.minimum(i + 1, n_tiles - 1),), memory_space=pltpu.SMEM),
            pl.BlockSpec((t, d), lambda i: (i, 0)),
            pl.BlockSpec((1, 6, d), lambda i: (i // tiles_per_b, 0, 0)),
            pl.BlockSpec((t, TOP_K), lambda i: (i, 0)),
            pl.BlockSpec((1, d), lambda i: (0, 0)),
            pl.BlockSpec(memory_space=pl.ANY),
        ],
        out_specs=pl.BlockSpec((t, d), lambda i: (i, 0)),
        out_shape=jax.ShapeDtypeStruct((n, d), F32),
        scratch_shapes=[
            pltpu.VMEM((2 * TOP_K * t * SUBLANES, LANES), F32),
            pltpu.SemaphoreType.DMA((2,)),
        ],
        compiler_params=_params(("arbitrary",)),
        name="moe_combine",
    )(dest_flat, dest_flat, x2, mod, wts, g_final.reshape(1, d), ys)


def _moe_layer(x2, mod, g_ffn, w_router, b_router, layer, w_gu, b_gu, w_dn, b_dn, seq, g_final, final_norm):
    n = x2.shape[0]
    h_tt, idx, wts, counts = _moe_router(x2, mod, g_ffn, w_router, b_router, seq)
    counts = counts[0, :N_EXPERTS].astype(I32)
    padded = (counts + MOE_BLOCK - 1) // MOE_BLOCK * MOE_BLOCK
    pad_end = jnp.cumsum(padded)
    pad_start = pad_end - padded
    n_rows = n * TOP_K + N_EXPERTS * MOE_BLOCK
    n_blocks = n_rows // MOE_BLOCK
    blk_start = jnp.arange(n_blocks, dtype=I32) * MOE_BLOCK
    block_e = jnp.minimum(jnp.sum((pad_end[None, :] <= blk_start[:, None]).astype(I32), axis=1),
                          N_EXPERTS - 1)
    n_active = (pad_end[-1:] // MOE_BLOCK).astype(I32)
    start_vec = jnp.zeros((1, LANES), F32).at[0, :N_EXPERTS].set(pad_start.astype(F32))
    dest = _moe_dest(idx, start_vec)
    xs = _moe_dispatch(h_tt, dest, pad_end.astype(I32), padded.astype(I32), n_rows)
    ys = _moe_experts(xs, block_e, n_active, layer, w_gu, b_gu, w_dn, b_dn)
    return _moe_combine(x2, mod, wts, dest, ys, seq, g_final, final_norm)


def _rope_kernel(pos_ref, inv_ref, cos_ref, sin_ref):
    ang = pos_ref[...] * inv_ref[...]
    cos_ref[...] = jnp.cos(ang)
    sin_ref[...] = jnp.sin(ang)


def _rope_table(positions):
    half = MLA_ROPE // 2
    n = positions.size
    rep = LANES // half
    inv = np.power(np.float32(ROPE_THETA),
                   -np.arange(half, dtype=np.float32) * np.float32(2.0 / MLA_ROPE)).astype(np.float32)
    inv_row = jnp.asarray(np.tile(inv, rep).reshape(1, LANES))
    pos = jnp.repeat(positions.reshape(n).astype(F32), half).reshape(n // rep, LANES)
    rows = n // rep
    tr = min(rows, 2048)
    cos, sin = pl.pallas_call(
        _rope_kernel,
        grid=(rows // tr,),
        in_specs=[pl.BlockSpec((tr, LANES), lambda i: (i, 0)), pl.BlockSpec((1, LANES), lambda i: (0, 0))],
        out_specs=[pl.BlockSpec((tr, LANES), lambda i: (i, 0))] * 2,
        out_shape=[jax.ShapeDtypeStruct((rows, LANES), F32)] * 2,
        compiler_params=_params(("arbitrary",)),
        name="rope_table",
    )(pos, inv_row)
    return cos.reshape(n, half), sin.reshape(n, half)


def _rope_halves(t1, t2, cos, sin):
    return t1 * cos - t2 * sin, t2 * cos + t1 * sin


def _kv_kernel(n_heads, x_ref, mod_ref, g_ref, wd_ref, gc_ref, wu_ref, cos_ref, sin_ref, k_ref, v_ref):
    half = MLA_ROPE // 2
    mod = mod_ref[0]
    hk = _norm_mod(x_ref[...], g_ref[...], mod[0:1], mod[1:2]).astype(BF16)
    ckr = _dot(hk, wd_ref[...])
    c_kv = _rms(ckr[:, :MLA_KV_RANK], gc_ref[...]).astype(BF16)
    r1, r2 = _rope_halves(ckr[:, MLA_KV_RANK:MLA_KV_RANK + half], ckr[:, MLA_KV_RANK + half:],
                          cos_ref[...], sin_ref[...])
    kv = _dot(c_kv, wu_ref[...])
    nv = n_heads * MLA_NOPE
    for h in range(n_heads):
        k_ref[0, h] = jnp.concatenate(
            [kv[:, h * MLA_NOPE:(h + 1) * MLA_NOPE], r1, r2], axis=-1).astype(BF16)
        v_ref[0, h] = kv[:, nv + h * MLA_V:nv + (h + 1) * MLA_V].astype(BF16)


def _kv_shared(x2, mod_kv, g_kv, w_dkv, g_ckv, w_ukv, cos, sin, bsz, seq):
    n, d = x2.shape
    n_heads = w_ukv.shape[1] // (MLA_NOPE + MLA_V)
    half = MLA_ROPE // 2
    tm = 512
    tiles = seq // tm
    w3 = w_ukv.reshape(MLA_KV_RANK, n_heads, MLA_NOPE + MLA_V)
    wu = jnp.concatenate([w3[:, :, :MLA_NOPE].reshape(MLA_KV_RANK, -1),
                          w3[:, :, MLA_NOPE:].reshape(MLA_KV_RANK, -1)], axis=1).astype(BF16)
    row = lambda b, j: (b * tiles + j, 0)
    const = lambda b, j: (0, 0)
    qk = MLA_NOPE + MLA_ROPE
    return pl.pallas_call(
        functools.partial(_kv_kernel, n_heads),
        grid=(bsz, tiles),
        in_specs=[
            pl.BlockSpec((tm, d), row),
            pl.BlockSpec((1, 2, d), lambda b, j: (b, 0, 0)),
            pl.BlockSpec((1, d), const),
            pl.BlockSpec(w_dkv.shape, const),
            pl.BlockSpec((1, MLA_KV_RANK), const),
            pl.BlockSpec(wu.shape, const),
            pl.BlockSpec((tm, half), row),
            pl.BlockSpec((tm, half), row),
        ],
        out_specs=[
            pl.BlockSpec((1, n_heads, tm, qk), lambda b, j: (b, 0, j, 0)),
            pl.BlockSpec((1, n_heads, tm, MLA_V), lambda b, j: (b, 0, j, 0)),
        ],
        out_shape=[
            jax.ShapeDtypeStruct((bsz, n_heads, seq, qk), BF16),
            jax.ShapeDtypeStruct((bsz, n_heads, seq, MLA_V), BF16),
        ],
        compiler_params=_params(("arbitrary", "arbitrary")),
        name="kv_shared",
    )(x2, mod_kv, g_kv.reshape(1, d), w_dkv.astype(BF16), g_ckv.reshape(1, MLA_KV_RANK), wu, cos, sin)


def _mla_q_kernel(n_heads, x_ref, mod_ref, g_ref, wd_ref, gc_ref, wu_ref, cos_ref, sin_ref, q_ref):
    half = MLA_ROPE // 2
    mod = mod_ref[0]
    h = _norm_mod(x_ref[...], g_ref[...], mod[0:1], mod[1:2]).astype(BF16)
    cq = _rms(_dot(h, wd_ref[...]), gc_ref[...]).astype(BF16)
    q = _dot(cq, wu_ref[...])
    n1 = n_heads * MLA_NOPE
    n2 = n1 + n_heads * half
    cos = jnp.concatenate([cos_ref[...]] * n_heads, axis=-1)
    sin = jnp.concatenate([sin_ref[...]] * n_heads, axis=-1)
    r1, r2 = _rope_halves(q[:, n1:n2], q[:, n2:], cos, sin)
    scale = math.log2(math.e) / math.sqrt(MLA_NOPE + MLA_ROPE)
    for hh in range(n_heads):
        q_ref[0, hh] = (jnp.concatenate(
            [q[:, hh * MLA_NOPE:(hh + 1) * MLA_NOPE],
             r1[:, hh * half:(hh + 1) * half], r2[:, hh * half:(hh + 1) * half]], axis=-1) * scale).astype(BF16)


def _mla_q(x2, mod, g_mix, w_dq, g_cq, w_uq, cos, sin, bsz, seq):
    n, d = x2.shape
    q_rank = w_dq.shape[1]
    qk = MLA_NOPE + MLA_ROPE
    n_heads = w_uq.shape[1] // qk
    half = MLA_ROPE // 2
    tm = 512
    tiles = seq // tm
    w3 = w_uq.reshape(q_rank, n_heads, qk)
    wu = jnp.concatenate([w3[:, :, :MLA_NOPE].reshape(q_rank, -1),
                          w3[:, :, MLA_NOPE:MLA_NOPE + half].reshape(q_rank, -1),
                          w3[:, :, MLA_NOPE + half:].reshape(q_rank, -1)], axis=1).astype(BF16)
    row = lambda b, j: (b * tiles + j, 0)
    const = lambda b, j: (0, 0)
    return pl.pallas_call(
        functools.partial(_mla_q_kernel, n_heads),
        grid=(bsz, tiles),
        in_specs=[
            pl.BlockSpec((tm, d), row),
            pl.BlockSpec((1, 6, d), lambda b, j: (b, 0, 0)),
            pl.BlockSpec((1, d), const),
            pl.BlockSpec(w_dq.shape, const),
            pl.BlockSpec((1, q_rank), const),
            pl.BlockSpec(wu.shape, const),
            pl.BlockSpec((tm, half), row),
            pl.BlockSpec((tm, half), row),
        ],
        out_specs=pl.BlockSpec((1, n_heads, tm, qk), lambda b, j: (b, 0, j, 0)),
        out_shape=jax.ShapeDtypeStruct((bsz, n_heads, seq, qk), BF16),
        compiler_params=_params(("arbitrary", "arbitrary")),
        name="mla_q",
    )(x2, mod, g_mix.reshape(1, d), w_dq.astype(BF16), g_cq.reshape(1, q_rank), wu, cos, sin)


ATTN_BLOCK = 512


def _flash_kernel(q_ref, k_ref, v_ref, o_ref):
    t = ATTN_BLOCK
    dv = v_ref.shape[3]
    n_q = q_ref.shape[2] // t

    def q_block(qi, carry_unused):
        q = q_ref[0, 0, pl.ds(pl.multiple_of(qi * t, t), t), :]

        def block(carry, start, width, diagonal):
            m_prev, l_prev, acc = carry
            k = k_ref[0, 0, pl.ds(start, width), :]
            v = v_ref[0, 0, pl.ds(start, width), :]
            s = _dot_nt(q, k)
            if diagonal:
                row = lax.broadcasted_iota(I32, (t, width), 0)
                col = lax.broadcasted_iota(I32, (t, width), 1)
                s = jnp.where(col <= row, s, NEG_BIG)
            m_new = jnp.maximum(m_prev, jnp.max(s, axis=-1, keepdims=True))
            alpha = jnp.exp2(m_prev - m_new)
            pr = jnp.exp2(s - m_new)
            l_new = alpha * l_prev + jnp.sum(pr, axis=-1, keepdims=True)
            return m_new, l_new, alpha * acc + _dot(pr.astype(BF16), v)

        init = (jnp.full((t, 1), NEG_BIG, F32), jnp.zeros((t, 1), F32), jnp.zeros((t, dv), F32))
        n_quads = lax.shift_right_logical(qi, 2)
        carry = lax.fori_loop(
            0, n_quads,
            lambda jj, c: block(c, pl.multiple_of(jj * (4 * t), 4 * t), 4 * t, False), init)
        carry = lax.cond(
            (qi & 2) == 2,
            lambda c: block(c, pl.multiple_of(n_quads * (4 * t), 2 * t), 2 * t, False),
            lambda c: c, carry)
        carry = lax.cond(
            (qi & 1) == 1,
            lambda c: block(c, pl.multiple_of((qi - 1) * t, t), t, False),
            lambda c: c, carry)
        _, l_fin, acc = block(carry, pl.multiple_of(qi * t, t), t, True)
        o_ref[0, pl.ds(pl.multiple_of(qi * t, t), t), :] = (acc / l_fin).astype(BF16)
        return carry_unused

    lax.fori_loop(0, n_q, q_block, 0)


def _flash_attn(q, k, v):
    bsz, n_heads, seq, qk = q.shape
    dv = v.shape[3]
    return pl.pallas_call(
        _flash_kernel,
        grid=(bsz, n_heads),
        in_specs=[
            pl.BlockSpec((1, 1, seq, qk), lambda b, h: (b, h, 0, 0)),
            pl.BlockSpec((1, 1, seq, qk), lambda b, h: (b, h, 0, 0)),
            pl.BlockSpec((1, 1, seq, dv), lambda b, h: (b, h, 0, 0)),
        ],
        out_specs=pl.BlockSpec((1, seq, dv), lambda b, h: (b, 0, h)),
        out_shape=jax.ShapeDtypeStruct((bsz, seq, n_heads * dv), BF16),
        compiler_params=_params(("arbitrary", "arbitrary")),
        name="flash_attn",
    )(q, k, v)


def _attn_out_kernel(x_ref, mod_ref, o_ref, w_ref, y_ref):
    y_ref[...] = x_ref[...] + mod_ref[0][2:3] * _dot(o_ref[...], w_ref[...])


def _attn_out(x2, mod, o2, w_o_bf, seq):
    n, d = x2.shape
    tm = 512
    tiles_per_b = seq // tm
    return pl.pallas_call(
        _attn_out_kernel,
        grid=(n // tm,),
        in_specs=[
            pl.BlockSpec((tm, d), lambda i: (i, 0)),
            pl.BlockSpec((1, 6, d), lambda i: (i // tiles_per_b, 0, 0)),
            pl.BlockSpec((tm, o2.shape[1]), lambda i: (i, 0)),
            pl.BlockSpec(w_o_bf.shape, lambda i: (0, 0)),
        ],
        out_specs=pl.BlockSpec((tm, d), lambda i: (i, 0)),
        out_shape=jax.ShapeDtypeStruct((n, d), F32),
        compiler_params=_params(("arbitrary",)),
        name="attn_out",
    )(x2, mod, o2, w_o_bf)


def _final_kernel(x_ref, g_ref, o_ref):
    o_ref[...] = _rms(x_ref[...], g_ref[...])


def _final_norm(x2, g):
    n, d = x2.shape
    tm = 1024
    return pl.pallas_call(
        _final_kernel,
        grid=(n // tm,),
        in_specs=[pl.BlockSpec((tm, d), lambda i: (i, 0)), pl.BlockSpec((1, d), lambda i: (0, 0))],
        out_specs=pl.BlockSpec((tm, d), lambda i: (i, 0)),
        out_shape=jax.ShapeDtypeStruct((n, d), F32),
        compiler_params=_params(("arbitrary",)),
        name="final_norm",
    )(x2, g.reshape(1, d))


def kernel(x, c, positions, g_mix, g_ffn, w_ada, b_ada, w_in_a, lb_logits, g_out_a, w_out_a, g_kv, w_ada_kv, b_ada_kv, w_dkv, g_ckv, w_ukv, w_dq, g_cq, w_uq, w_o_b, w_router, b_router, w_gu, b_gu, w_dn, b_dn, g_final):
    bsz, seq, d = x.shape
    depth = g_mix.shape[0]
    n_a = w_in_a.shape[0]
    n = bsz * seq
    x2 = x.reshape(n, d)

    mods = _ada_mod(c, w_ada, b_ada)
    mods = mods.reshape(depth, bsz, 6, d)
    mod_kv = _ada_mod(c, w_ada_kv[None], b_ada_kv[None]).reshape(bsz, 2, d)

    shared = None
    cos = sin = None
    for l in range(depth):
        mod = mods[l]
        if l < n_a:
            q, k, v, lf, gate = _hgrn_in(x2, mod, g_mix[l], lb_logits, w_in_a[l].astype(BF16), l, seq)
            x2 = _hgrn_rec(x2, mod, q, k, v, lf, gate, g_out_a[l], w_out_a[l].astype(BF16), bsz, seq)
        else:
            j = l - n_a
            qh = _mla_q(x2, mod, g_mix[l], w_dq[j], g_cq[j], w_uq[j], cos, sin, bsz, seq)
            o2 = _flash_attn(qh, *shared)
            x2 = _attn_out(x2, mod, o2.reshape(n, -1), w_o_b[j].astype(BF16), seq)
        fuse_final = l == depth - 1 and l != n_a - 1
        x2 = _moe_layer(x2, mod, g_ffn[l], w_router[l], b_router[l],
                        l, w_gu, b_gu, w_dn, b_dn, seq, g_final, fuse_final)
        if l == n_a - 1:
            cos, sin = _rope_table(positions)
            shared = _kv_shared(x2, mod_kv, g_kv, w_dkv, g_ckv, w_ukv, cos, sin, bsz, seq)
    if not fuse_final:
        x2 = _final_norm(x2, g_final)
    return x2.reshape(bsz, seq, d)
```
